```python
import jax, jax.numpy as jnp
from jax import lax
import numpy as np

D_MODEL = 1024
BATCH = 8
SEQ = 2048
DEPTH = 2

HEAD_SIZE = 64
D_RWKV = D_MODEL
N_HEADS_RWKV = D_RWKV // HEAD_SIZE
DECAY_RANK = 64
AAA_RANK = 64
GATE_RANK = 128
VRES_RANK = 32
D_CONV = D_MODEL
CONV_WIDTH = 31
D_FF = 4 * D_MODEL

RMS_EPS = 1e-6
LN_EPS = 1e-5
GN_EPS = 64e-5

N_SHIFT = 3 * D_RWKV + DECAY_RANK + AAA_RANK + GATE_RANK
N_COLS = N_SHIFT + 2 * D_CONV + D_RWKV + D_CONV
RWKV_SPLITS = [D_RWKV, 2 * D_RWKV, 3 * D_RWKV, 3 * D_RWKV + DECAY_RANK, 3 * D_RWKV + DECAY_RANK + AAA_RANK]

kernel_name = "rwkv7_conformer_gated_hybrid"


def _rms_norm(x, g):
    xf = x.astype(jnp.float32)
    y = xf * lax.rsqrt(jnp.mean(xf * xf, axis=-1, keepdims=True) + RMS_EPS)
    return y.astype(x.dtype) * g


def _layer_norm(z, g, b):
    zf = z.astype(jnp.float32)
    mu = jnp.mean(zf, axis=-1, keepdims=True)
    var = jnp.mean(jnp.square(zf - mu), axis=-1, keepdims=True)
    return ((zf - mu) * lax.rsqrt(var + LN_EPS)).astype(z.dtype) * g + b


def _token_shift(p, mu):
    p_prev = jnp.pad(p, ((0, 0), (1, 0), (0, 0)))[:, :-1]
    return p + (p_prev - p) * mu


def _rwkv7_scan(r, decay, k, v, a, b):
    bsz, _, h, n = r.shape

    def step(S, inp):
        r_t, w_t, k_t, v_t, a_t, b_t = inp
        sa = jnp.einsum('bhij,bhj->bhi', S, a_t)
        S = S * w_t[:, :, None, :] + sa[..., None] * b_t[:, :, None, :] + v_t[..., None] * k_t[:, :, None, :]
        return S, jnp.einsum('bhij,bhj->bhi', S, r_t)

    seq_major = tuple(jnp.moveaxis(t, 1, 0) for t in (r, decay, k, v, a, b))
    s0 = jnp.zeros((bsz, h, n, n), jnp.float32)
    _, y = lax.scan(step, s0, seq_major)
    return jnp.moveaxis(y, 0, 1)


def _rwkv7_time_mix(xs, v_first, vres, w0, w_decay_up, a0, w_aaa_up, w_gate_up,
                    k_k, k_a, r_k, gn_gain, gn_bias):
    bsz, t, _ = xs.shape
    r, k, v, w_lo, a_lo, g_lo = jnp.split(xs, RWKV_SPLITS, axis=-1)
    w = -jax.nn.softplus(-(w0 + jnp.tanh(w_lo) @ w_decay_up)) - 0.5
    a = jax.nn.sigmoid(a0 + a_lo @ w_aaa_up)
    g = jax.nn.sigmoid(g_lo) @ w_gate_up
    if v_first is None:
        v_first = v
    else:
        v_lo, v0, w_vres_up = vres
        v = v + (v_first - v) * jax.nn.sigmoid(v0 + v_lo @ w_vres_up)
    heads = lambda z: z.reshape(bsz, t, N_HEADS_RWKV, HEAD_SIZE).astype(jnp.float32)
    kk = heads(k * k_k)
    kk = kk / jnp.maximum(jnp.sqrt(jnp.sum(kk * kk, axis=-1, keepdims=True)), 1e-12)
    k = k * (1.0 + (a - 1.0) * k_a)
    rh, kh, vh, ah = heads(r), heads(k), heads(v), heads(a)
    decay = jnp.exp(-jnp.exp(heads(w)))
    y = _rwkv7_scan(rh, decay, kh, vh, -kk, kk * ah)
    mu = jnp.mean(y, axis=-1, keepdims=True)
    var = jnp.mean(jnp.square(y - mu), axis=-1, keepdims=True)
    y = (y - mu) * lax.rsqrt(var + GN_EPS)
    y = y.reshape(bsz, t, D_RWKV).astype(xs.dtype) * gn_gain + gn_bias
    bonus = (jnp.sum(rh * kh * r_k, axis=-1, keepdims=True) * vh).reshape(bsz, t, D_RWKV).astype(xs.dtype)
    return (y + bonus) * g, v_first


def _conformer_conv(u, conv_w, conv_b, ln_gain, ln_bias):
    glu = u[..., :D_CONV] * jax.nn.sigmoid(u[..., D_CONV:])
    z = lax.conv_general_dilated(
        glu, conv_w[:, None, :].astype(glu.dtype), window_strides=(1,),
        padding=((CONV_WIDTH - 1, 0),), dimension_numbers=('NWC', 'WIO', 'NWC'),
        feature_group_count=D_CONV) + conv_b
    return jax.nn.silu(_layer_norm(z, ln_gain, ln_bias))


def setup_inputs(seed: int = 0) -> dict:
    key = jax.random.key(seed)
    ks = iter(jax.random.split(key, 40))
    L, LV = DEPTH, DEPTH - 1

    def nrm(shape, s):
        return jax.random.normal(next(ks), shape, jnp.float32) * s

    def unif(shape, lo, hi):
        return jax.random.uniform(next(ks), shape, jnp.float32, lo, hi)

    return {
        "x": nrm((BATCH, SEQ, D_MODEL), 1.0),
        "c": nrm((BATCH, D_MODEL), 1.0),
        "norm_mix_gain": 1.0 + nrm((L, D_MODEL), 0.02),
        "norm_ffn_gain": 1.0 + nrm((L, D_MODEL), 0.02),
        "ada_w": nrm((L, D_MODEL, 6 * D_MODEL), 0.3 * D_MODEL ** -0.5),
        "ada_b": nrm((L, 6 * D_MODEL), 0.01),
        "w_in": nrm((L, D_MODEL, N_COLS), D_MODEL ** -0.5),
        "w_in_vres": nrm((LV, D_MODEL, VRES_RANK), D_MODEL ** -0.5),
        "mu_shift": unif((L, N_SHIFT), 0.0, 1.0),
        "mu_vres": unif((LV, VRES_RANK), 0.0, 1.0),
        "w0": unif((L, D_RWKV), -6.0, -1.0),
        "w_decay_up": nrm((L, DECAY_RANK, D_RWKV), 0.5 * DECAY_RANK ** -0.5),
        "a0": nrm((L, D_RWKV), 0.1),
        "w_aaa_up": nrm((L, AAA_RANK, D_RWKV), AAA_RANK ** -0.5),
        "w_gate_up": nrm((L, GATE_RANK, D_RWKV), GATE_RANK ** -0.5),
        "k_k": 0.85 + nrm((L, D_RWKV), 0.05),
        "k_a": 1.0 + nrm((L, D_RWKV), 0.05),
        "r_k": nrm((L, N_HEADS_RWKV, HEAD_SIZE), 0.1),
        "gn_gain": 1.0 + nrm((L, D_RWKV), 0.02),
        "gn_bias": nrm((L, D_RWKV), 0.01),
        "v0": nrm((LV, D_RWKV), 0.1),
        "w_vres_up": nrm((LV, VRES_RANK, D_RWKV), VRES_RANK ** -0.5),
        "conv_w": nrm((L, CONV_WIDTH, D_CONV), CONV_WIDTH ** -0.5),
        "conv_b": nrm((L, D_CONV), 0.01),
        "conv_ln_gain": 1.0 + nrm((L, D_CONV), 0.02),
        "conv_ln_bias": nrm((L, D_CONV), 0.01),
        "w_out": nrm((L, D_RWKV + D_CONV, D_MODEL), (D_RWKV + D_CONV) ** -0.5),
        "w_ff_in": nrm((L, D_MODEL, D_FF), D_MODEL ** -0.5),
        "w_ff_out": nrm((L, D_FF, D_MODEL), D_FF ** -0.5),
        "final_gain": 1.0 + nrm((D_MODEL,), 0.02),
    }


def reference(x, c, norm_mix_gain, norm_ffn_gain, ada_w, ada_b, w_in, w_in_vres, mu_shift, mu_vres,
              w0, w_decay_up, a0, w_aaa_up, w_gate_up, k_k, k_a, r_k, gn_gain, gn_bias, v0, w_vres_up,
              conv_w, conv_b, conv_ln_gain, conv_ln_bias, w_out, w_ff_in, w_ff_out, final_gain):
    c_act = jax.nn.silu(c)
    v_first = None
    for l in range(DEPTH):
        mod = c_act @ ada_w[l] + ada_b[l]
        sh_m, sc_m, gt_m, sh_f, sc_f, gt_f = [m[:, None, :] for m in jnp.split(mod, 6, axis=-1)]

        h = _rms_norm(x, norm_mix_gain[l]) * (1.0 + sc_m) + sh_m
        w_comb = w_in[l] if l == 0 else jnp.concatenate([w_in[l], w_in_vres[l - 1]], axis=1)
        proj = h @ w_comb
        rwkv_in = _token_shift(proj[..., :N_SHIFT], mu_shift[l])
        conv_in = proj[..., N_SHIFT:N_SHIFT + 2 * D_CONV]
        gates = jax.nn.sigmoid(proj[..., N_SHIFT + 2 * D_CONV:N_COLS])
        vres = None if l == 0 else (_token_shift(proj[..., N_COLS:], mu_vres[l - 1]), v0[l - 1], w_vres_up[l - 1])
        y_a, v_first = _rwkv7_time_mix(rwkv_in, v_first, vres, w0[l], w_decay_up[l], a0[l], w_aaa_up[l],
                                       w_gate_up[l], k_k[l], k_a[l], r_k[l], gn_gain[l], gn_bias[l])
        y_b = _conformer_conv(conv_in, conv_w[l], conv_b[l], conv_ln_gain[l], conv_ln_bias[l])
        merged = jnp.concatenate([y_a, y_b], axis=-1) * gates
        x = x + gt_m * (merged @ w_out[l])

        h = _rms_norm(x, norm_ffn_gain[l]) * (1.0 + sc_f) + sh_f
        x = x + gt_f * (jnp.square(jax.nn.relu(h @ w_ff_in[l])) @ w_ff_out[l])
    return _rms_norm(x, final_gain)
```

```python
import functools

import jax
import jax.numpy as jnp
from jax import lax
from jax.experimental import pallas as pl
from jax.experimental.pallas import tpu as pltpu

F32 = jnp.float32
BF16 = jnp.bfloat16

V7X_LANES = 128
V7X_VMEM_BYTES = 64 * 1024 * 1024

HEAD_SIZE = 64
HEADS_PER_SLAB = V7X_LANES // HEAD_SIZE
CHUNK = 64
DECAY_RANK = 64
AAA_RANK = 64
GATE_RANK = 128
VRES_RANK = 32
LO_WIDTH = 3 * V7X_LANES
CONV_WIDTH = 31
CONV_HALO = 32

RMS_EPS = 1e-6
LN_EPS = 1e-5
GN_EPS = 64e-5

ROW_TILE = 256
MOD_COL_TILE = 1536
VMEM_LIMIT = 56 * 1024 * 1024


def _dot(a, b):
    return jnp.dot(a, b, preferred_element_type=F32)


def _dot_nt(a, b):
    return lax.dot_general(a, b, (((1,), (1,)), ((), ())), preferred_element_type=F32)


def _dot_tn(a, b):
    return lax.dot_general(a, b, (((0,), (0,)), ((), ())), preferred_element_type=F32)


def _split3(x):
    hi = x.astype(BF16)
    r1 = x - hi.astype(F32)
    mid = r1.astype(BF16)
    lo = (r1 - mid.astype(F32)).astype(BF16)
    return hi, mid, lo


def _const_spec(shape):
    nd = len(shape)
    return pl.BlockSpec(shape, lambda *_: (0,) * nd, pipeline_mode=pl.Buffered(1))


def _mod_kernel(c_ref, w_ref, b_ref, o_ref):
    c = c_ref[...]
    ca = c * jax.nn.sigmoid(c)
    ch, cm, _ = _split3(ca)
    wh, wm, _ = _split3(w_ref[0])
    o_ref[0] = _dot(ch, wh) + _dot(ch, wm) + _dot(cm, wh) + b_ref[0]


def _modulation(c, ada_w, ada_b):
    depth, d, n = ada_w.shape
    bsz = c.shape[0]
    return pl.pallas_call(
        _mod_kernel,
        grid=(depth, n // MOD_COL_TILE),
        in_specs=[
            pl.BlockSpec((bsz, d), lambda l, j: (0, 0)),
            pl.BlockSpec((1, d, MOD_COL_TILE), lambda l, j: (l, 0, j)),
            pl.BlockSpec((1, 1, MOD_COL_TILE), lambda l, j: (l, 0, j)),
        ],
        out_specs=pl.BlockSpec((1, bsz, MOD_COL_TILE), lambda l, j: (l, 0, j)),
        out_shape=jax.ShapeDtypeStruct((depth, bsz, n), F32),
        compiler_params=pltpu.CompilerParams(dimension_semantics=("arbitrary", "arbitrary")),
        name="adaln_modulation",
    )(c, ada_w, ada_b.reshape(depth, 1, n))


def _inproj_kernel(x_ref, mod_ref, gain_ref, mu_ref, w_ref, rkv_ref, lo_ref, glu_ref, gate_ref, carry_ref,
                   *, tiles_per_seq, d):
    tm = x_ref.shape[0]
    n_shift = rkv_ref.shape[1] + lo_ref.shape[1]

    @pl.when(pl.program_id(0) % tiles_per_seq == 0)
    def _():
        carry_ref[...] = jnp.zeros_like(carry_ref)

    x = x_ref[...]
    mod = mod_ref[0]
    shift, scale = mod[:, 0:d], mod[:, d:2 * d]
    ms = jnp.mean(x * x, axis=-1, keepdims=True)
    h = (x * lax.rsqrt(ms + RMS_EPS)) * gain_ref[...] * (1.0 + scale) + shift
    hb = h.astype(BF16)
    first_row = lax.broadcasted_iota(jnp.int32, (tm, 1), 0) == 0

    def shifted(col0, width, out_ref, out_col0):
        p = _dot(hb, w_ref[:, col0:col0 + width])
        prev = jnp.where(first_row, carry_ref[0:1, col0:col0 + width], pltpu.roll(p, 1, 0))
        carry_ref[0:1, col0:col0 + width] = p[tm - 1:tm, :]
        out_ref[:, out_col0:out_col0 + width] = p + (prev - p) * mu_ref[:, col0:col0 + width]

    n_rkv = rkv_ref.shape[1]
    for j in range(n_rkv // d):
        shifted(j * d, d, rkv_ref, j * d)
    shifted(n_rkv, lo_ref.shape[1], lo_ref, 0)

    u1 = _dot(hb, w_ref[:, n_shift:n_shift + d])
    u2 = _dot(hb, w_ref[:, n_shift + d:n_shift + 2 * d])
    glu_ref[...] = u1 * jax.nn.sigmoid(u2)
    g0 = n_shift + 2 * d
    for j in range(gate_ref.shape[1] // d):
        gate_ref[:, j * d:(j + 1) * d] = jax.nn.sigmoid(_dot(hb, w_ref[:, g0 + j * d:g0 + (j + 1) * d]))


def _in_projection(x2, mod_l, gain, mu_all, w_all, *, seq, d, n_rkv, n_gate):
    rows = x2.shape[0]
    tm = min(ROW_TILE, seq)
    tiles_per_seq = seq // tm
    n_all = w_all.shape[1]
    row_spec = lambda width: pl.BlockSpec((tm, width), lambda i: (i, 0))
    return pl.pallas_call(
        functools.partial(_inproj_kernel, tiles_per_seq=tiles_per_seq, d=d),
        grid=(rows // tm,),
        in_specs=[
            row_spec(d),
            pl.BlockSpec((1, 1, mod_l.shape[-1]), lambda i: (i // tiles_per_seq, 0, 0)),
            _const_spec((1, d)),
            _const_spec((1, n_rkv + LO_WIDTH)),
            _const_spec((d, n_all)),
        ],
        out_specs=[row_spec(n_rkv), row_spec(LO_WIDTH), row_spec(d), row_spec(n_gate)],
        out_shape=[
            jax.ShapeDtypeStruct((rows, n_rkv), F32),
            jax.ShapeDtypeStruct((rows, LO_WIDTH), F32),
            jax.ShapeDtypeStruct((rows, d), F32),
            jax.ShapeDtypeStruct((rows, n_gate), F32),
        ],
        scratch_shapes=[pltpu.VMEM((8, n_rkv + LO_WIDTH), F32)],
        compiler_params=pltpu.CompilerParams(dimension_semantics=("arbitrary",), vmem_limit_bytes=VMEM_LIMIT),
        name="in_projection",
    )(x2, mod_l, gain, mu_all, w_all)


def _stack_heads(x, lane_is_head0):
    return jnp.concatenate([jnp.where(lane_is_head0, x, 0.0), jnp.where(lane_is_head0, 0.0, x)], axis=0)


def _rwkv_kernel(*refs, d, has_vres):
    if has_vres:
        (rkv_ref, lo_ref, vfirst_ref, pvec_ref, wd_ref, wa_ref, wg_ref, wv_ref, tri_ref, ones_ref,
         out_ref, state_ref, y_ref) = refs
    else:
        (rkv_ref, lo_ref, pvec_ref, wd_ref, wa_ref, wg_ref, tri_ref, ones_ref,
         out_ref, state_ref, y_ref) = refs
    c = CHUNK
    n_slab = d // V7X_LANES

    @pl.when(pl.program_id(1) == 0)
    def _():
        state_ref[...] = jnp.zeros_like(state_ref)

    w0, a0, k_k, k_a = (pvec_ref[i:i + 1, :] for i in range(4))
    r_k, gn_gain, gn_bias, v0 = (pvec_ref[i:i + 1, :] for i in range(4, 8))

    def head_sum(x):
        xs = jnp.concatenate([x[:, s * V7X_LANES:(s + 1) * V7X_LANES] for s in range(n_slab)], axis=0)
        ss = _dot(xs.astype(BF16), ones_ref[...])
        return jnp.concatenate([ss[s * c:(s + 1) * c, :] for s in range(n_slab)], axis=1)

    r = rkv_ref[:, 0:d]
    k = rkv_ref[:, d:2 * d]
    v = rkv_ref[:, 2 * d:3 * d]
    lo_wa = lo_ref[:, 0:V7X_LANES]
    w_pre = w0 + _dot(jnp.tanh(lo_wa).astype(BF16), wd_ref[...])
    a_sig = jax.nn.sigmoid(a0 + _dot(lo_wa.astype(BF16), wa_ref[...]))
    gate = _dot(jax.nn.sigmoid(lo_ref[:, V7X_LANES:2 * V7X_LANES]).astype(BF16), wg_ref[...])
    if has_vres:
        mix = jax.nn.sigmoid(v0 + _dot(lo_ref[:, 2 * V7X_LANES:3 * V7X_LANES].astype(BF16), wv_ref[...]))
        v = v + (vfirst_ref[...] - v) * mix

    z = -w_pre
    softplus = jnp.maximum(z, 0.0) + jnp.log1p(jnp.exp(-jnp.abs(z)))
    logw = -jnp.exp(-softplus - 0.5)

    kk = k * k_k
    kk = kk / jnp.maximum(jnp.sqrt(head_sum(kk * kk)), 1e-12)
    k = k * (1.0 + (a_sig - 1.0) * k_a)
    a_vec = -kk
    b_vec = kk * a_sig

    tri = tri_ref[...]
    lh, lm, ll = _split3(logw)
    cum = _dot(tri, lh) + _dot(tri, lm) + _dot(tri, ll)
    tot = cum[c - 1:c, :]
    e_inc = jnp.exp(cum)
    e_inv = jnp.exp(-cum)
    e_hat = jnp.exp(tot - cum)
    r_t = r * e_inc
    a_t = a_vec * jnp.exp(cum - logw)
    b_t = b_vec * e_inv
    k_t = k * e_inv
    b_h = b_vec * e_hat
    k_h = k * e_hat
    g_tot = jnp.exp(tot)

    lane_is_head0 = lax.broadcasted_iota(jnp.int32, (c, V7X_LANES), 1) < HEAD_SIZE
    ri = lax.broadcasted_iota(jnp.int32, (2 * c, 2 * c), 0) & (c - 1)
    ci = lax.broadcasted_iota(jnp.int32, (2 * c, 2 * c), 1) & (c - 1)
    strict = ri > ci
    incl = ri >= ci
    eye = (lax.broadcasted_iota(jnp.int32, (2 * c, 2 * c), 0) ==
           lax.broadcasted_iota(jnp.int32, (2 * c, 2 * c), 1)).astype(F32)

    for s in range(n_slab):
        sl = slice(s * V7X_LANES, (s + 1) * V7X_LANES)
        stack = lambda x: _stack_heads(x[:, sl], lane_is_head0).astype(BF16)
        a_s, r_s, b_s, k_s = stack(a_t), stack(r_t), stack(b_t), stack(k_t)
        v_s, bh_s, kh_s = stack(v), stack(b_h), stack(k_h)

        zz = _dot_nt(jnp.concatenate([a_s, r_s], axis=0), jnp.concatenate([b_s, k_s], axis=0))
        n2 = 2 * c
        a_ab = jnp.where(strict, zz[0:n2, 0:n2], 0.0)
        a_ak = jnp.where(strict, zz[0:n2, n2:2 * n2], 0.0)
        m_rb = jnp.where(incl, zz[n2:2 * n2, 0:n2], 0.0)
        m_rk = jnp.where(incl, zz[n2:2 * n2, n2:2 * n2], 0.0)

        p = a_ab.astype(BF16)
        t_inv = eye + a_ab
        p = _dot(p, p)
        steps = CHUNK.bit_length() - 2
        for it in range(steps):
            pb = p.astype(BF16)
            if it + 1 < steps:
                both = _dot(pb, jnp.concatenate([t_inv.astype(BF16), pb], axis=1))
                t_inv = t_inv + both[:, 0:n2]
                p = both[:, n2:2 * n2]
            else:
                t_inv = t_inv + _dot(pb, t_inv.astype(BF16))
        tb = t_inv.astype(BF16)

        akv = _dot(a_ak.astype(BF16), v_s)
        wu = _dot(tb, jnp.concatenate([a_s, akv.astype(BF16)], axis=1))
        w_tld = wu[:, 0:V7X_LANES]
        u_tld = wu[:, V7X_LANES:2 * V7X_LANES]

        state = state_ref[s]
        sb = state.astype(BF16)
        u_s = _dot_nt(w_tld.astype(BF16), sb) + u_tld
        ub = u_s.astype(BF16)
        y_s = (_dot_nt(r_s, sb) + _dot(m_rb.astype(BF16), ub) + _dot(m_rk.astype(BF16), v_s))
        y_ref[:, sl] = y_s[0:c, :] + y_s[c:2 * c, :]
        state_ref[s] = state * g_tot[:, sl] + _dot_tn(jnp.concatenate([ub, v_s], axis=0),
                                                     jnp.concatenate([bh_s, kh_s], axis=0))

    y = y_ref[...]
    inv_n = 1.0 / HEAD_SIZE
    mu = head_sum(y) * inv_n
    yc = y - mu
    var = head_sum(yc * yc) * inv_n
    yn = yc * lax.rsqrt(var + GN_EPS) * gn_gain + gn_bias
    bonus = head_sum(r * k * r_k) * v
    out_ref[...] = (yn + bonus) * gate


def _rwkv_time_mix(rkv, lo, v_first_src, pvec, wd, wa, wg, wv, tri, ones_bd, *, bsz, seq, d):
    has_vres = v_first_src is not None
    chunks = seq // CHUNK
    n_slab = d // V7X_LANES
    row_map = lambda b, t: (b * chunks + t, 0)
    in_specs = [pl.BlockSpec((CHUNK, 3 * d), row_map), pl.BlockSpec((CHUNK, LO_WIDTH), row_map)]
    args = [rkv, lo]
    if has_vres:
        in_specs.append(pl.BlockSpec((CHUNK, d), lambda b, t: (b * chunks + t, 2)))
        args.append(v_first_src)
    consts = [pvec, wd, wa, wg] + ([wv] if has_vres else []) + [tri, ones_bd]
    in_specs += [_const_spec(a.shape) for a in consts]
    args += consts
    return pl.pallas_call(
        functools.partial(_rwkv_kernel, d=d, has_vres=has_vres),
        grid=(bsz, chunks),
        in_specs=in_specs,
        out_specs=pl.BlockSpec((CHUNK, d), row_map),
        out_shape=jax.ShapeDtypeStruct((bsz * seq, d), F32),
        scratch_shapes=[pltpu.VMEM((n_slab, V7X_LANES, V7X_LANES), F32), pltpu.VMEM((CHUNK, d), F32)],
        compiler_params=pltpu.CompilerParams(dimension_semantics=("arbitrary", "arbitrary"),
                                             vmem_limit_bytes=VMEM_LIMIT),
        name="rwkv7_time_mix",
    )(*args)


def _mix_mlp_kernel(ya_ref, glu_ref, gate_ref, x_ref, mod_ref, cw_ref, cvec_ref, wout_ref, w1_ref, w2_ref,
                    out_ref, hist_ref, yb_ref, *, tiles_per_seq, d, final_norm):
    tm = x_ref.shape[0]
    n_slab = d // V7X_LANES
    row_group = 32

    @pl.when(pl.program_id(0) % tiles_per_seq == 0)
    def _():
        hist_ref[0:CONV_HALO, :] = jnp.zeros((CONV_HALO, d), F32)

    hist_ref[CONV_HALO:CONV_HALO + tm, :] = glu_ref[...]
    conv_b, ln_gain, ln_bias = (cvec_ref[i:i + 1, :] for i in range(3))
    ffn_gain, final_gain = cvec_ref[3:4, :], cvec_ref[4:5, :]

    base = CONV_HALO - (CONV_WIDTH - 1)
    for s in range(n_slab):
        sl = slice(s * V7X_LANES, (s + 1) * V7X_LANES)
        for g in range(tm // row_group):
            r0 = g * row_group
            acc = jnp.broadcast_to(conv_b[:, sl], (row_group, V7X_LANES))
            for j in range(CONV_WIDTH):
                acc = acc + cw_ref[j:j + 1, sl] * hist_ref[base + r0 + j:base + r0 + j + row_group, sl]
            yb_ref[r0:r0 + row_group, sl] = acc
    hist_ref[0:CONV_HALO, :] = hist_ref[tm:tm + CONV_HALO, :]

    z = yb_ref[...]
    mu = jnp.mean(z, axis=-1, keepdims=True)
    zc = z - mu
    var = jnp.mean(zc * zc, axis=-1, keepdims=True)
    zn = zc * lax.rsqrt(var + LN_EPS) * ln_gain + ln_bias
    y_b = zn * jax.nn.sigmoid(zn)

    mod = mod_ref[0]
    gt_m = mod[:, 2 * d:3 * d]
    sh_f, sc_f, gt_f = mod[:, 3 * d:4 * d], mod[:, 4 * d:5 * d], mod[:, 5 * d:6 * d]
    m_a = (ya_ref[...] * gate_ref[:, 0:d]).astype(BF16)
    m_b = (y_b * gate_ref[:, d:2 * d]).astype(BF16)
    x1 = x_ref[...] + gt_m * (_dot(m_a, wout_ref[0:d, :]) + _dot(m_b, wout_ref[d:2 * d, :]))

    ms = jnp.mean(x1 * x1, axis=-1, keepdims=True)
    h = ((x1 * lax.rsqrt(ms + RMS_EPS)) * ffn_gain * (1.0 + sc_f) + sh_f).astype(BF16)
    d_ff = w1_ref.shape[1]
    acc = jnp.zeros((tm, d), F32)
    for j in range(d_ff // d):
        hid = jnp.maximum(_dot(h, w1_ref[:, j * d:(j + 1) * d]), 0.0)
        acc = acc + _dot((hid * hid).astype(BF16), w2_ref[j * d:(j + 1) * d, :])
    x2 = x1 + gt_f * acc
    if final_norm:
        ms2 = jnp.mean(x2 * x2, axis=-1, keepdims=True)
        x2 = (x2 * lax.rsqrt(ms2 + RMS_EPS)) * final_gain
    out_ref[...] = x2


def _mix_mlp(ya, glu, gates, x2, mod_l, cw, cvec, wout, w1, w2, *, seq, d, final_norm):
    rows = x2.shape[0]
    tm = min(ROW_TILE, seq)
    tiles_per_seq = seq // tm
    row_spec = lambda width: pl.BlockSpec((tm, width), lambda i: (i, 0))
    return pl.pallas_call(
        functools.partial(_mix_mlp_kernel, tiles_per_seq=tiles_per_seq, d=d, final_norm=final_norm),
        grid=(rows // tm,),
        in_specs=[
            row_spec(d), row_spec(d), row_spec(2 * d), row_spec(d),
            pl.BlockSpec((1, 1, mod_l.shape[-1]), lambda i: (i // tiles_per_seq, 0, 0)),
            _const_spec(cw.shape), _const_spec(cvec.shape),
            _const_spec(wout.shape), _const_spec(w1.shape), _const_spec(w2.shape),
        ],
        out_specs=row_spec(d),
        out_shape=jax.ShapeDtypeStruct((rows, d), F32),
        scratch_shapes=[pltpu.VMEM((CONV_HALO + tm, d), F32), pltpu.VMEM((tm, d), F32)],
        compiler_params=pltpu.CompilerParams(dimension_semantics=("arbitrary",), vmem_limit_bytes=VMEM_LIMIT),
        name="conv_merge_mlp",
    )(ya, glu, gates, x2, mod_l, cw, cvec, wout, w1, w2)


def _pad_rows(w, rows, row0=0):
    out = jnp.zeros((rows, w.shape[1]), w.dtype)
    return lax.dynamic_update_slice(out, w, (row0, 0))


def kernel(x, c, norm_mix_gain, norm_ffn_gain, ada_w, ada_b, w_in, w_in_vres, mu_shift, mu_vres, w0, w_decay_up, a0, w_aaa_up, w_gate_up, k_k, k_a, r_k, gn_gain, gn_bias, v0, w_vres_up, conv_w, conv_b, conv_ln_gain, conv_ln_bias, w_out, w_ff_in, w_ff_out, final_gain):
    bsz, seq, d = x.shape
    depth = w_in.shape[0]
    n_rkv = 3 * d
    n_lo = DECAY_RANK + AAA_RANK + GATE_RANK
    n_shift = n_rkv + n_lo
    n_gate = 2 * d
    assert d % V7X_LANES == 0 and seq % CHUNK == 0 and seq % min(ROW_TILE, seq) == 0
    assert w_in.shape[2] == n_shift + 2 * d + n_gate and ada_w.shape[2] % MOD_COL_TILE == 0

    mod = _modulation(c, ada_w, ada_b)
    tri = jnp.tril(jnp.ones((CHUNK, CHUNK), BF16))
    lane = jnp.arange(V7X_LANES) // HEAD_SIZE
    ones_bd = (lane[:, None] == lane[None, :]).astype(BF16)

    x2 = x.reshape(bsz * seq, d)
    v_first_src = None
    for l in range(depth):
        has_vres = l > 0
        lo_pad = LO_WIDTH - n_lo
        if has_vres:
            w_tail = jnp.pad(w_in_vres[l - 1], ((0, 0), (0, lo_pad - VRES_RANK)))
            mu_tail = jnp.pad(mu_vres[l - 1], (0, lo_pad - VRES_RANK))
        else:
            w_tail = jnp.zeros((d, lo_pad), F32)
            mu_tail = jnp.zeros((lo_pad,), F32)
        w_all = jnp.concatenate([w_in[l][:, :n_shift], w_tail, w_in[l][:, n_shift:]], axis=1).astype(BF16)
        mu_all = jnp.concatenate([mu_shift[l], mu_tail]).reshape(1, n_shift + lo_pad)
        mod_l = mod[l].reshape(bsz, 1, 6 * d)

        rkv, lo, glu, gates = _in_projection(x2, mod_l, norm_mix_gain[l].reshape(1, d), mu_all, w_all,
                                             seq=seq, d=d, n_rkv=n_rkv, n_gate=n_gate)

        pvec = jnp.stack([w0[l], a0[l], k_k[l], k_a[l], r_k[l].reshape(d), gn_gain[l], gn_bias[l],
                          v0[l - 1] if has_vres else jnp.zeros((d,), F32)])
        wd = _pad_rows(w_decay_up[l], V7X_LANES, 0).astype(BF16)
        wa = _pad_rows(w_aaa_up[l], V7X_LANES, DECAY_RANK).astype(BF16)
        wg = w_gate_up[l].astype(BF16)
        wv = _pad_rows(w_vres_up[l - 1], V7X_LANES, 0).astype(BF16) if has_vres else None
        ya = _rwkv_time_mix(rkv, lo, v_first_src, pvec, wd, wa, wg, wv, tri, ones_bd, bsz=bsz, seq=seq, d=d)
        if l == 0:
            v_first_src = rkv

        cw = _pad_rows(conv_w[l], CONV_HALO)
        cvec = jnp.stack([conv_b[l], conv_ln_gain[l], conv_ln_bias[l], norm_ffn_gain[l], final_gain,
                          jnp.zeros((d,), F32), jnp.zeros((d,), F32), jnp.zeros((d,), F32)])
        x2 = _mix_mlp(ya, glu, gates, x2, mod_l, cw, cvec, w_out[l].astype(BF16), w_ff_in[l].astype(BF16),
                      w_ff_out[l].astype(BF16), seq=seq, d=d, final_norm=(l == depth - 1))
    return x2.reshape(bsz, seq, d)
```

```python
import functools

import jax
import jax.numpy as jnp
from jax import lax
from jax.experimental import pallas as pl
from jax.experimental.pallas import tpu as pltpu

F32 = jnp.float32
BF16 = jnp.bfloat16

V7X_LANES = 128
V7X_VMEM_BYTES = 64 * 1024 * 1024

HEAD_SIZE = 64
HEADS_PER_SLAB = V7X_LANES // HEAD_SIZE
CHUNK = 64
DECAY_RANK = 64
AAA_RANK = 64
GATE_RANK = 128
VRES_RANK = 32
LO_WIDTH = 3 * V7X_LANES
CONV_WIDTH = 31
CONV_HALO = 32

RMS_EPS = 1e-6
LN_EPS = 1e-5
GN_EPS = 64e-5

ROW_TILE = 256
MOD_COL_TILE = 1536
VMEM_LIMIT = 56 * 1024 * 1024


def _dot(a, b):
    return jnp.dot(a, b, preferred_element_type=F32)


def _dot_nt(a, b):
    return lax.dot_general(a, b, (((1,), (1,)), ((), ())), preferred_element_type=F32)


def _dot_tn(a, b):
    return lax.dot_general(a, b, (((0,), (0,)), ((), ())), preferred_element_type=F32)


def _split3(x):
    hi = x.astype(BF16)
    r1 = x - hi.astype(F32)
    mid = r1.astype(BF16)
    lo = (r1 - mid.astype(F32)).astype(BF16)
    return hi, mid, lo


def _const_spec(shape):
    nd = len(shape)
    return pl.BlockSpec(shape, lambda *_: (0,) * nd, pipeline_mode=pl.Buffered(1))


def _mod_kernel(c_ref, w_ref, b_ref, o_ref):
    c = c_ref[...]
    ca = c * jax.nn.sigmoid(c)
    ch, cm, _ = _split3(ca)
    wh, wm, _ = _split3(w_ref[0])
    o_ref[0] = _dot(ch, wh) + _dot(ch, wm) + _dot(cm, wh) + b_ref[0]


def _modulation(c, ada_w, ada_b):
    depth, d, n = ada_w.shape
    bsz = c.shape[0]
    return pl.pallas_call(
        _mod_kernel,
        grid=(depth, n // MOD_COL_TILE),
        in_specs=[
            pl.BlockSpec((bsz, d), lambda l, j: (0, 0)),
            pl.BlockSpec((1, d, MOD_COL_TILE), lambda l, j: (l, 0, j)),
            pl.BlockSpec((1, 1, MOD_COL_TILE), lambda l, j: (l, 0, j)),
        ],
        out_specs=pl.BlockSpec((1, bsz, MOD_COL_TILE), lambda l, j: (l, 0, j)),
        out_shape=jax.ShapeDtypeStruct((depth, bsz, n), F32),
        compiler_params=pltpu.CompilerParams(dimension_semantics=("arbitrary", "arbitrary")),
        name="adaln_modulation",
    )(c, ada_w, ada_b.reshape(depth, 1, n))


def _inproj_kernel(x_ref, mod_ref, gain_ref, mu_ref, w_ref, rkv_ref, lo_ref, glu_ref, gate_ref, carry_ref,
                   *, tiles_per_seq, d):
    tm = x_ref.shape[0]
    n_shift = rkv_ref.shape[1] + lo_ref.shape[1]

    @pl.when(pl.program_id(0) % tiles_per_seq == 0)
    def _():
        carry_ref[...] = jnp.zeros_like(carry_ref)

    x = x_ref[...]
    mod = mod_ref[0]
    shift, scale = mod[:, 0:d], mod[:, d:2 * d]
    ms = jnp.mean(x * x, axis=-1, keepdims=True)
    h = (x * lax.rsqrt(ms + RMS_EPS)) * gain_ref[...] * (1.0 + scale) + shift
    hb = h.astype(BF16)
    first_row = lax.broadcasted_iota(jnp.int32, (tm, 1), 0) == 0

    def shifted(col0, width, out_ref, out_col0):
        p = _dot(hb, w_ref[:, col0:col0 + width])
        prev = jnp.where(first_row, carry_ref[0:1, col0:col0 + width], pltpu.roll(p, 1, 0))
        carry_ref[0:1, col0:col0 + width] = p[tm - 1:tm, :]
        out_ref[:, out_col0:out_col0 + width] = p + (prev - p) * mu_ref[:, col0:col0 + width]

    n_rkv = rkv_ref.shape[1]
    for j in range(n_rkv // d):
        shifted(j * d, d, rkv_ref, j * d)
    shifted(n_rkv, lo_ref.shape[1], lo_ref, 0)

    u1 = _dot(hb, w_ref[:, n_shift:n_shift + d])
    u2 = _dot(hb, w_ref[:, n_shift + d:n_shift + 2 * d])
    glu_ref[...] = u1 * jax.nn.sigmoid(u2)
    g0 = n_shift + 2 * d
    for j in range(gate_ref.shape[1] // d):
        gate_ref[:, j * d:(j + 1) * d] = jax.nn.sigmoid(_dot(hb, w_ref[:, g0 + j * d:g0 + (j + 1) * d]))


def _in_projection(x2, mod_l, gain, mu_all, w_all, *, seq, d, n_rkv, n_gate):
    rows = x2.shape[0]
    tm = min(ROW_TILE, seq)
    tiles_per_seq = seq // tm
    n_all = w_all.shape[1]
    row_spec = lambda width: pl.BlockSpec((tm, width), lambda i: (i, 0))
    return pl.pallas_call(
        functools.partial(_inproj_kernel, tiles_per_seq=tiles_per_seq, d=d),
        grid=(rows // tm,),
        in_specs=[
            row_spec(d),
            pl.BlockSpec((1, 1, mod_l.shape[-1]), lambda i: (i // tiles_per_seq, 0, 0)),
            _const_spec((1, d)),
            _const_spec((1, n_rkv + LO_WIDTH)),
            _const_spec((d, n_all)),
        ],
        out_specs=[row_spec(n_rkv), row_spec(LO_WIDTH), row_spec(d), row_spec(n_gate)],
        out_shape=[
            jax.ShapeDtypeStruct((rows, n_rkv), F32),
            jax.ShapeDtypeStruct((rows, LO_WIDTH), F32),
            jax.ShapeDtypeStruct((rows, d), F32),
            jax.ShapeDtypeStruct((rows, n_gate), F32),
        ],
        scratch_shapes=[pltpu.VMEM((8, n_rkv + LO_WIDTH), F32)],
        compiler_params=pltpu.CompilerParams(dimension_semantics=("arbitrary",), vmem_limit_bytes=VMEM_LIMIT),
        name="in_projection",
    )(x2, mod_l, gain, mu_all, w_all)


def _stack_heads(x, lane_is_head0):
    return jnp.concatenate([jnp.where(lane_is_head0, x, 0.0), jnp.where(lane_is_head0, 0.0, x)], axis=0)


def _rwkv_kernel(*refs, d, has_vres):
    if has_vres:
        (rkv_ref, lo_ref, vfirst_ref, pvec_ref, wd_ref, wa_ref, wg_ref, wv_ref, tri_ref, ones_ref,
         out_ref, state_ref, y_ref) = refs
    else:
        (rkv_ref, lo_ref, pvec_ref, wd_ref, wa_ref, wg_ref, tri_ref, ones_ref,
         out_ref, state_ref, y_ref) = refs
    c = CHUNK
    n_slab = d // V7X_LANES

    @pl.when(pl.program_id(1) == 0)
    def _():
        state_ref[...] = jnp.zeros_like(state_ref)

    w0, a0, k_k, k_a = (pvec_ref[i:i + 1, :] for i in range(4))
    r_k, gn_gain, gn_bias, v0 = (pvec_ref[i:i + 1, :] for i in range(4, 8))

    def head_sum(x):
        xs = jnp.concatenate([x[:, s * V7X_LANES:(s + 1) * V7X_LANES] for s in range(n_slab)], axis=0)
        ss = _dot(xs.astype(BF16), ones_ref[...])
        return jnp.concatenate([ss[s * c:(s + 1) * c, :] for s in range(n_slab)], axis=1)

    r = rkv_ref[:, 0:d]
    k = rkv_ref[:, d:2 * d]
    v = rkv_ref[:, 2 * d:3 * d]
    lo_wa = lo_ref[:, 0:V7X_LANES]
    w_pre = w0 + _dot(jnp.tanh(lo_wa).astype(BF16), wd_ref[...])
    a_sig = jax.nn.sigmoid(a0 + _dot(lo_wa.astype(BF16), wa_ref[...]))
    gate = _dot(jax.nn.sigmoid(lo_ref[:, V7X_LANES:2 * V7X_LANES]).astype(BF16), wg_ref[...])
    if has_vres:
        mix = jax.nn.sigmoid(v0 + _dot(lo_ref[:, 2 * V7X_LANES:3 * V7X_LANES].astype(BF16), wv_ref[...]))
        v = v + (vfirst_ref[...] - v) * mix

    z = -w_pre
    softplus = jnp.maximum(z, 0.0) + jnp.log1p(jnp.exp(-jnp.abs(z)))
    logw = -jnp.exp(-softplus - 0.5)

    kk = k * k_k
    kk = kk / jnp.maximum(jnp.sqrt(head_sum(kk * kk)), 1e-12)
    k = k * (1.0 + (a_sig - 1.0) * k_a)
    a_vec = -kk
    b_vec = kk * a_sig

    tri = tri_ref[...]
    lh, lm, ll = _split3(logw)
    cum = _dot(tri, lh) + _dot(tri, lm) + _dot(tri, ll)
    tot = cum[c - 1:c, :]
    e_inc = jnp.exp(cum)
    e_inv = jnp.exp(-cum)
    e_hat = jnp.exp(tot - cum)
    r_t = r * e_inc
    a_t = a_vec * jnp.exp(cum - logw)
    b_t = b_vec * e_inv
    k_t = k * e_inv
    b_h = b_vec * e_hat
    k_h = k * e_hat
    g_tot = jnp.exp(tot)

    lane_is_head0 = lax.broadcasted_iota(jnp.int32, (c, V7X_LANES), 1) < HEAD_SIZE
    ri = lax.broadcasted_iota(jnp.int32, (2 * c, 2 * c), 0) & (c - 1)
    ci = lax.broadcasted_iota(jnp.int32, (2 * c, 2 * c), 1) & (c - 1)
    strict = ri > ci
    incl = ri >= ci
    eye = (lax.broadcasted_iota(jnp.int32, (2 * c, 2 * c), 0) ==
           lax.broadcasted_iota(jnp.int32, (2 * c, 2 * c), 1)).astype(F32)

    slabs = range(n_slab)
    n2 = 2 * c
    lanes = [slice(s * V7X_LANES, (s + 1) * V7X_LANES) for s in slabs]
    stack = lambda x: [_stack_heads(x[:, sl], lane_is_head0).astype(BF16) for sl in lanes]
    a_s, r_s, b_s, k_s = stack(a_t), stack(r_t), stack(b_t), stack(k_t)
    v_s, bh_s, kh_s = stack(v), stack(b_h), stack(k_h)

    zz = [_dot_nt(jnp.concatenate([a_s[s], r_s[s]], axis=0), jnp.concatenate([b_s[s], k_s[s]], axis=0))
          for s in slabs]
    a_ab = [jnp.where(strict, z[0:n2, 0:n2], 0.0) for z in zz]
    a_ak = [jnp.where(strict, z[0:n2, n2:2 * n2], 0.0).astype(BF16) for z in zz]
    m_rb = [jnp.where(incl, z[n2:2 * n2, 0:n2], 0.0).astype(BF16) for z in zz]
    m_rk = [jnp.where(incl, z[n2:2 * n2, n2:2 * n2], 0.0).astype(BF16) for z in zz]

    pb = [a.astype(BF16) for a in a_ab]
    t_inv = [eye + a for a in a_ab]
    p = [_dot(q, q) for q in pb]
    akv = [_dot(a_ak[s], v_s[s]).astype(BF16) for s in slabs]
    sb = [state_ref[s].astype(BF16) for s in slabs]
    y_rs = [_dot_nt(r_s[s], sb[s]) + _dot(m_rk[s], v_s[s]) for s in slabs]
    steps = CHUNK.bit_length() - 2
    for it in range(steps):
        pb = [q.astype(BF16) for q in p]
        if it + 1 < steps:
            both = [_dot(pb[s], jnp.concatenate([t_inv[s].astype(BF16), pb[s]], axis=1)) for s in slabs]
            t_inv = [t_inv[s] + both[s][:, 0:n2] for s in slabs]
            p = [b2[:, n2:2 * n2] for b2 in both]
        else:
            t_inv = [t_inv[s] + _dot(pb[s], t_inv[s].astype(BF16)) for s in slabs]

    wu = [_dot(t_inv[s].astype(BF16), jnp.concatenate([a_s[s], akv[s]], axis=1)) for s in slabs]
    ub = [(_dot_nt(wu[s][:, 0:V7X_LANES].astype(BF16), sb[s]) + wu[s][:, V7X_LANES:2 * V7X_LANES]).astype(BF16)
          for s in slabs]
    for s in slabs:
        y_s = y_rs[s] + _dot(m_rb[s], ub[s])
        y_ref[:, lanes[s]] = y_s[0:c, :] + y_s[c:2 * c, :]
    for s in slabs:
        state_ref[s] = state_ref[s] * g_tot[:, lanes[s]] + _dot_tn(
            jnp.concatenate([ub[s], v_s[s]], axis=0), jnp.concatenate([bh_s[s], kh_s[s]], axis=0))

    y = y_ref[...]
    inv_n = 1.0 / HEAD_SIZE
    mu = head_sum(y) * inv_n
    yc = y - mu
    var = head_sum(yc * yc) * inv_n
    yn = yc * lax.rsqrt(var + GN_EPS) * gn_gain + gn_bias
    bonus = head_sum(r * k * r_k) * v
    out_ref[...] = (yn + bonus) * gate


def _rwkv_time_mix(rkv, lo, v_first_src, pvec, wd, wa, wg, wv, tri, ones_bd, *, bsz, seq, d):
    has_vres = v_first_src is not None
    chunks = seq // CHUNK
    n_slab = d // V7X_LANES
    row_map = lambda b, t: (b * chunks + t, 0)
    in_specs = [pl.BlockSpec((CHUNK, 3 * d), row_map), pl.BlockSpec((CHUNK, LO_WIDTH), row_map)]
    args = [rkv, lo]
    if has_vres:
        in_specs.append(pl.BlockSpec((CHUNK, d), lambda b, t: (b * chunks + t, 2)))
        args.append(v_first_src)
    consts = [pvec, wd, wa, wg] + ([wv] if has_vres else []) + [tri, ones_bd]
    in_specs += [_const_spec(a.shape) for a in consts]
    args += consts
    return pl.pallas_call(
        functools.partial(_rwkv_kernel, d=d, has_vres=has_vres),
        grid=(bsz, chunks),
        in_specs=in_specs,
        out_specs=pl.BlockSpec((CHUNK, d), row_map),
        out_shape=jax.ShapeDtypeStruct((bsz * seq, d), F32),
        scratch_shapes=[pltpu.VMEM((n_slab, V7X_LANES, V7X_LANES), F32), pltpu.VMEM((CHUNK, d), F32)],
        compiler_params=pltpu.CompilerParams(dimension_semantics=("arbitrary", "arbitrary"),
                                             vmem_limit_bytes=VMEM_LIMIT),
        name="rwkv7_time_mix",
    )(*args)


def _mix_mlp_kernel(ya_ref, glu_ref, gate_ref, x_ref, mod_ref, cw_ref, cvec_ref, wout_ref, w1_ref, w2_ref,
                    out_ref, hist_ref, yb_ref, *, tiles_per_seq, d, final_norm):
    tm = x_ref.shape[0]
    n_slab = d // V7X_LANES
    row_group = 32

    @pl.when(pl.program_id(0) % tiles_per_seq == 0)
    def _():
        hist_ref[0:CONV_HALO, :] = jnp.zeros((CONV_HALO, d), F32)

    hist_ref[CONV_HALO:CONV_HALO + tm, :] = glu_ref[...]
    conv_b, ln_gain, ln_bias = (cvec_ref[i:i + 1, :] for i in range(3))
    ffn_gain, final_gain = cvec_ref[3:4, :], cvec_ref[4:5, :]

    base = CONV_HALO - (CONV_WIDTH - 1)
    for s in range(n_slab):
        sl = slice(s * V7X_LANES, (s + 1) * V7X_LANES)
        for g in range(tm // row_group):
            r0 = g * row_group
            acc = jnp.broadcast_to(conv_b[:, sl], (row_group, V7X_LANES))
            for j in range(CONV_WIDTH):
                acc = acc + cw_ref[j:j + 1, sl] * hist_ref[base + r0 + j:base + r0 + j + row_group, sl]
            yb_ref[r0:r0 + row_group, sl] = acc
    hist_ref[0:CONV_HALO, :] = hist_ref[tm:tm + CONV_HALO, :]

    z = yb_ref[...]
    mu = jnp.mean(z, axis=-1, keepdims=True)
    zc = z - mu
    var = jnp.mean(zc * zc, axis=-1, keepdims=True)
    zn = zc * lax.rsqrt(var + LN_EPS) * ln_gain + ln_bias
    y_b = zn * jax.nn.sigmoid(zn)

    mod = mod_ref[0]
    gt_m = mod[:, 2 * d:3 * d]
    sh_f, sc_f, gt_f = mod[:, 3 * d:4 * d], mod[:, 4 * d:5 * d], mod[:, 5 * d:6 * d]
    m_a = (ya_ref[...] * gate_ref[:, 0:d]).astype(BF16)
    m_b = (y_b * gate_ref[:, d:2 * d]).astype(BF16)
    x1 = x_ref[...] + gt_m * (_dot(m_a, wout_ref[0:d, :]) + _dot(m_b, wout_ref[d:2 * d, :]))

    ms = jnp.mean(x1 * x1, axis=-1, keepdims=True)
    h = ((x1 * lax.rsqrt(ms + RMS_EPS)) * ffn_gain * (1.0 + sc_f) + sh_f).astype(BF16)
    d_ff = w1_ref.shape[1]
    acc = jnp.zeros((tm, d), F32)
    for j in range(d_ff // d):
        hid = jnp.maximum(_dot(h, w1_ref[:, j * d:(j + 1) * d]), 0.0)
        acc = acc + _dot((hid * hid).astype(BF16), w2_ref[j * d:(j + 1) * d, :])
    x2 = x1 + gt_f * acc
    if final_norm:
        ms2 = jnp.mean(x2 * x2, axis=-1, keepdims=True)
        x2 = (x2 * lax.rsqrt(ms2 + RMS_EPS)) * final_gain
    out_ref[...] = x2


def _mix_mlp(ya, glu, gates, x2, mod_l, cw, cvec, wout, w1, w2, *, seq, d, final_norm):
    rows = x2.shape[0]
    tm = min(ROW_TILE, seq)
    tiles_per_seq = seq // tm
    row_spec = lambda width: pl.BlockSpec((tm, width), lambda i: (i, 0))
    return pl.pallas_call(
        functools.partial(_mix_mlp_kernel, tiles_per_seq=tiles_per_seq, d=d, final_norm=final_norm),
        grid=(rows // tm,),
        in_specs=[
            row_spec(d), row_spec(d), row_spec(2 * d), row_spec(d),
            pl.BlockSpec((1, 1, mod_l.shape[-1]), lambda i: (i // tiles_per_seq, 0, 0)),
            _const_spec(cw.shape), _const_spec(cvec.shape),
            _const_spec(wout.shape), _const_spec(w1.shape), _const_spec(w2.shape),
        ],
        out_specs=row_spec(d),
        out_shape=jax.ShapeDtypeStruct((rows, d), F32),
        scratch_shapes=[pltpu.VMEM((CONV_HALO + tm, d), F32), pltpu.VMEM((tm, d), F32)],
        compiler_params=pltpu.CompilerParams(dimension_semantics=("arbitrary",), vmem_limit_bytes=VMEM_LIMIT),
        name="conv_merge_mlp",
    )(ya, glu, gates, x2, mod_l, cw, cvec, wout, w1, w2)


def _pad_rows(w, rows, row0=0):
    out = jnp.zeros((rows, w.shape[1]), w.dtype)
    return lax.dynamic_update_slice(out, w, (row0, 0))


def kernel(x, c, norm_mix_gain, norm_ffn_gain, ada_w, ada_b, w_in, w_in_vres, mu_shift, mu_vres, w0, w_decay_up, a0, w_aaa_up, w_gate_up, k_k, k_a, r_k, gn_gain, gn_bias, v0, w_vres_up, conv_w, conv_b, conv_ln_gain, conv_ln_bias, w_out, w_ff_in, w_ff_out, final_gain):
    bsz, seq, d = x.shape
    depth = w_in.shape[0]
    n_rkv = 3 * d
    n_lo = DECAY_RANK + AAA_RANK + GATE_RANK
    n_shift = n_rkv + n_lo
    n_gate = 2 * d
    assert d % V7X_LANES == 0 and seq % CHUNK == 0 and seq % min(ROW_TILE, seq) == 0
    assert w_in.shape[2] == n_shift + 2 * d + n_gate and ada_w.shape[2] % MOD_COL_TILE == 0

    mod = _modulation(c, ada_w, ada_b)
    tri = jnp.tril(jnp.ones((CHUNK, CHUNK), BF16))
    lane = jnp.arange(V7X_LANES) // HEAD_SIZE
    ones_bd = (lane[:, None] == lane[None, :]).astype(BF16)

    x2 = x.reshape(bsz * seq, d)
    v_first_src = None
    for l in range(depth):
        has_vres = l > 0
        lo_pad = LO_WIDTH - n_lo
        if has_vres:
            w_tail = jnp.pad(w_in_vres[l - 1], ((0, 0), (0, lo_pad - VRES_RANK)))
            mu_tail = jnp.pad(mu_vres[l - 1], (0, lo_pad - VRES_RANK))
        else:
            w_tail = jnp.zeros((d, lo_pad), F32)
            mu_tail = jnp.zeros((lo_pad,), F32)
        w_all = jnp.concatenate([w_in[l][:, :n_shift], w_tail, w_in[l][:, n_shift:]], axis=1).astype(BF16)
        mu_all = jnp.concatenate([mu_shift[l], mu_tail]).reshape(1, n_shift + lo_pad)
        mod_l = mod[l].reshape(bsz, 1, 6 * d)

        rkv, lo, glu, gates = _in_projection(x2, mod_l, norm_mix_gain[l].reshape(1, d), mu_all, w_all,
                                             seq=seq, d=d, n_rkv=n_rkv, n_gate=n_gate)

        pvec = jnp.stack([w0[l], a0[l], k_k[l], k_a[l], r_k[l].reshape(d), gn_gain[l], gn_bias[l],
                          v0[l - 1] if has_vres else jnp.zeros((d,), F32)])
        wd = _pad_rows(w_decay_up[l], V7X_LANES, 0).astype(BF16)
        wa = _pad_rows(w_aaa_up[l], V7X_LANES, DECAY_RANK).astype(BF16)
        wg = w_gate_up[l].astype(BF16)
        wv = _pad_rows(w_vres_up[l - 1], V7X_LANES, 0).astype(BF16) if has_vres else None
        ya = _rwkv_time_mix(rkv, lo, v_first_src, pvec, wd, wa, wg, wv, tri, ones_bd, bsz=bsz, seq=seq, d=d)
        if l == 0:
            v_first_src = rkv

        cw = _pad_rows(conv_w[l], CONV_HALO)
        cvec = jnp.stack([conv_b[l], conv_ln_gain[l], conv_ln_bias[l], norm_ffn_gain[l], final_gain,
                          jnp.zeros((d,), F32), jnp.zeros((d,), F32), jnp.zeros((d,), F32)])
        x2 = _mix_mlp(ya, glu, gates, x2, mod_l, cw, cvec, w_out[l].astype(BF16), w_ff_in[l].astype(BF16),
                      w_ff_out[l].astype(BF16), seq=seq, d=d, final_norm=(l == depth - 1))
    return x2.reshape(bsz, seq, d)
```

```python
import functools

import jax
import jax.numpy as jnp
from jax import lax
from jax.experimental import pallas as pl
from jax.experimental.pallas import tpu as pltpu

F32 = jnp.float32
BF16 = jnp.bfloat16

V7X_LANES = 128
V7X_VMEM_BYTES = 64 * 1024 * 1024
V7X_MXU_WIDTH = 256

HEAD_SIZE = 64
HEADS_PER_SLAB = V7X_LANES // HEAD_SIZE
CHUNK = 64
CHUNKS_PER_STEP = 4
DECAY_RANK = 64
AAA_RANK = 64
GATE_RANK = 128
VRES_RANK = 32
LO_WIDTH = 3 * V7X_LANES
CONV_WIDTH = 31
CONV_HALO = 32
CONV_ROWS = 64
CONV_GROUPS = 4

RMS_EPS = 1e-6
LN_EPS = 1e-5
GN_EPS = 64e-5

ROW_TILE = 256
MOD_COL_TILE = 1536
VMEM_LIMIT = 56 * 1024 * 1024


def _dot(a, b):
    return jnp.dot(a, b, preferred_element_type=F32)


def _dot_nt(a, b):
    return lax.dot_general(a, b, (((1,), (1,)), ((), ())), preferred_element_type=F32)


def _dot_tn(a, b):
    return lax.dot_general(a, b, (((0,), (0,)), ((), ())), preferred_element_type=F32)


def _split3(x):
    hi = x.astype(BF16)
    r1 = x - hi.astype(F32)
    mid = r1.astype(BF16)
    lo = (r1 - mid.astype(F32)).astype(BF16)
    return hi, mid, lo


def _const_spec(shape):
    nd = len(shape)
    return pl.BlockSpec(shape, lambda *_: (0,) * nd, pipeline_mode=pl.Buffered(1))


def _mod_kernel(c_ref, w_ref, b_ref, o_ref):
    c = c_ref[...]
    ca = c * jax.nn.sigmoid(c)
    ch, cm, _ = _split3(ca)
    wh, wm, _ = _split3(w_ref[0])
    o_ref[0] = _dot(ch, wh) + _dot(ch, wm) + _dot(cm, wh) + b_ref[0]


def _modulation(c, ada_w, ada_b):
    depth, d, n = ada_w.shape
    bsz = c.shape[0]
    return pl.pallas_call(
        _mod_kernel,
        grid=(depth, n // MOD_COL_TILE),
        in_specs=[
            pl.BlockSpec((bsz, d), lambda l, j: (0, 0)),
            pl.BlockSpec((1, d, MOD_COL_TILE), lambda l, j: (l, 0, j)),
            pl.BlockSpec((1, 1, MOD_COL_TILE), lambda l, j: (l, 0, j)),
        ],
        out_specs=pl.BlockSpec((1, bsz, MOD_COL_TILE), lambda l, j: (l, 0, j)),
        out_shape=jax.ShapeDtypeStruct((depth, bsz, n), F32),
        compiler_params=pltpu.CompilerParams(dimension_semantics=("arbitrary", "arbitrary")),
        name="adaln_modulation",
    )(c, ada_w, ada_b.reshape(depth, 1, n))


def _causal_conv_slab(hist_ref, cw_ref, cb_ref, out_ref, sl, tm):
    base = CONV_HALO - (CONV_WIDTH - 1)
    n = CONV_ROWS + CONV_HALO
    for g in range(tm // CONV_ROWS):
        r0 = g * CONV_ROWS
        x = hist_ref[r0:r0 + n, sl]
        acc = jnp.broadcast_to(cb_ref[:, sl], (CONV_ROWS, V7X_LANES))
        for res in range(8):
            xr = x if res == 0 else pltpu.roll(x, n - res, 0)
            for j in range(CONV_WIDTH):
                if (base + j) % 8 == res:
                    q0 = base + j - res
                    acc = acc + cw_ref[j:j + 1, sl] * xr[q0:q0 + CONV_ROWS, :]
        out_ref[r0:r0 + CONV_ROWS, sl] = acc


def _inproj_kernel(x_ref, mod_ref, gain_ref, mu_ref, cw_ref, cb_ref, w_ref, rkv_ref, lo_ref, z_ref, gate_ref,
                   carry_ref, hist_ref, *, tiles_per_seq, d):
    tm = x_ref.shape[0]
    n_shift = rkv_ref.shape[1] + lo_ref.shape[1]

    @pl.when(pl.program_id(0) % tiles_per_seq == 0)
    def _():
        carry_ref[...] = jnp.zeros_like(carry_ref)
        hist_ref[0:CONV_HALO, :] = jnp.zeros((CONV_HALO, d), F32)

    x = x_ref[...]
    mod = mod_ref[0]
    shift, scale = mod[:, 0:d], mod[:, d:2 * d]
    ms = jnp.mean(x * x, axis=-1, keepdims=True)
    h = (x * lax.rsqrt(ms + RMS_EPS)) * gain_ref[...] * (1.0 + scale) + shift
    hb = h.astype(BF16)
    first_row = lax.broadcasted_iota(jnp.int32, (tm, 1), 0) == 0

    def shifted(col0, width, out_ref, out_col0):
        p = _dot(hb, w_ref[:, col0:col0 + width])
        prev = jnp.where(first_row, carry_ref[0:1, col0:col0 + width], pltpu.roll(p, 1, 0))
        carry_ref[0:1, col0:col0 + width] = p[tm - 1:tm, :]
        out_ref[:, out_col0:out_col0 + width] = p + (prev - p) * mu_ref[:, col0:col0 + width]

    def gate(j):
        g0 = n_shift + 2 * d + j * d
        gate_ref[:, j * d:(j + 1) * d] = jax.nn.sigmoid(_dot(hb, w_ref[:, g0:g0 + d]))

    def glu(block):
        lo_col, width = block.start * V7X_LANES, len(block) * V7X_LANES
        u1 = _dot(hb, w_ref[:, n_shift + lo_col:n_shift + lo_col + width])
        u2 = _dot(hb, w_ref[:, n_shift + d + lo_col:n_shift + d + lo_col + width])
        hist_ref[CONV_HALO:CONV_HALO + tm, lo_col:lo_col + width] = u1 * jax.nn.sigmoid(u2)

    def conv(block):
        for s in block:
            _causal_conv_slab(hist_ref, cw_ref, cb_ref, z_ref, slice(s * V7X_LANES, (s + 1) * V7X_LANES), tm)

    n_rkv = rkv_ref.shape[1]
    n_slab = d // V7X_LANES
    assert n_rkv == 3 * d and gate_ref.shape[1] == 2 * d and n_slab % CONV_GROUPS == 0
    per = n_slab // CONV_GROUPS
    blocks = [range(i * per, (i + 1) * per) for i in range(CONV_GROUPS)]
    always = pl.program_id(0) >= 0
    glu(range(n_slab))

    @pl.when(always)
    def _():
        shifted(0, d, rkv_ref, 0)
        conv(blocks[0])

    @pl.when(always)
    def _():
        shifted(d, d, rkv_ref, d)
        conv(blocks[1])

    @pl.when(always)
    def _():
        shifted(2 * d, d, rkv_ref, 2 * d)
        conv(blocks[2])

    @pl.when(always)
    def _():
        shifted(n_rkv, lo_ref.shape[1], lo_ref, 0)
        gate(0)
        conv(blocks[3])
        gate(1)
        hist_ref[0:CONV_HALO, :] = hist_ref[tm:tm + CONV_HALO, :]


def _in_projection(x2, mod_l, gain, mu_all, cw, cb, w_all, *, seq, d, n_rkv, n_gate):
    rows = x2.shape[0]
    tm = min(ROW_TILE, seq)
    tiles_per_seq = seq // tm
    n_all = w_all.shape[1]
    row_spec = lambda width: pl.BlockSpec((tm, width), lambda i: (i, 0))
    return pl.pallas_call(
        functools.partial(_inproj_kernel, tiles_per_seq=tiles_per_seq, d=d),
        grid=(rows // tm,),
        in_specs=[
            row_spec(d),
            pl.BlockSpec((1, 1, mod_l.shape[-1]), lambda i: (i // tiles_per_seq, 0, 0)),
            _const_spec((1, d)),
            _const_spec((1, n_rkv + LO_WIDTH)),
            _const_spec(cw.shape),
            _const_spec((1, d)),
            _const_spec((d, n_all)),
        ],
        out_specs=[row_spec(n_rkv), row_spec(LO_WIDTH), row_spec(d), row_spec(n_gate)],
        out_shape=[
            jax.ShapeDtypeStruct((rows, n_rkv), F32),
            jax.ShapeDtypeStruct((rows, LO_WIDTH), F32),
            jax.ShapeDtypeStruct((rows, d), F32),
            jax.ShapeDtypeStruct((rows, n_gate), F32),
        ],
        scratch_shapes=[pltpu.VMEM((8, n_rkv + LO_WIDTH), F32), pltpu.VMEM((CONV_HALO + tm, d), F32)],
        compiler_params=pltpu.CompilerParams(dimension_semantics=("arbitrary",), vmem_limit_bytes=VMEM_LIMIT),
        name="in_projection",
    )(x2, mod_l, gain, mu_all, cw, cb, w_all)


def _stack_heads(x, lane_is_head0):
    return jnp.concatenate([jnp.where(lane_is_head0, x, 0.0), jnp.where(lane_is_head0, 0.0, x)], axis=0)


def _rwkv_kernel(*refs, d, has_vres):
    if has_vres:
        (rkv_ref, lo_ref, vfirst_ref, pvec_ref, wd_ref, wa_ref, wg_ref, wv_ref, tri_ref, ones_ref,
         out_ref, state_ref, y_ref) = refs
    else:
        (rkv_ref, lo_ref, pvec_ref, wd_ref, wa_ref, wg_ref, tri_ref, ones_ref,
         out_ref, state_ref, y_ref) = refs
    c = CHUNK
    n2 = 2 * c
    n_slab = d // V7X_LANES
    slabs = range(n_slab)
    lanes = [slice(s * V7X_LANES, (s + 1) * V7X_LANES) for s in slabs]

    @pl.when(pl.program_id(1) == 0)
    def _():
        state_ref[...] = jnp.zeros_like(state_ref)

    w0, a0, k_k, k_a = (pvec_ref[i:i + 1, :] for i in range(4))
    r_k, gn_gain, gn_bias, v0 = (pvec_ref[i:i + 1, :] for i in range(4, 8))
    lane_is_head0 = lax.broadcasted_iota(jnp.int32, (c, V7X_LANES), 1) < HEAD_SIZE
    ri = lax.broadcasted_iota(jnp.int32, (n2, n2), 0) & (c - 1)
    ci = lax.broadcasted_iota(jnp.int32, (n2, n2), 1) & (c - 1)
    strict = ri > ci
    incl = ri >= ci
    eye = (lax.broadcasted_iota(jnp.int32, (n2, n2), 0) ==
           lax.broadcasted_iota(jnp.int32, (n2, n2), 1)).astype(F32)

    def head_sum(x):
        wide = ones_ref.shape[0]
        xs = jnp.concatenate([x[:, j:j + wide] for j in range(0, d, wide)], axis=0)
        ss = _dot(xs.astype(BF16), ones_ref[...])
        return jnp.concatenate([ss[j * c:(j + 1) * c, :] for j in range(d // wide)], axis=1)

    def prep(rows, p):
        r = rkv_ref[rows, 0:d]
        k = rkv_ref[rows, d:2 * d]
        v = rkv_ref[rows, 2 * d:3 * d]
        lo_wa = lo_ref[rows, 0:V7X_LANES]
        w_pre = w0 + _dot(jnp.tanh(lo_wa).astype(BF16), wd_ref[...])
        a_sig = jax.nn.sigmoid(a0 + _dot(lo_wa.astype(BF16), wa_ref[...]))
        p["gate"] = _dot(jax.nn.sigmoid(lo_ref[rows, V7X_LANES:2 * V7X_LANES]).astype(BF16), wg_ref[...])
        if has_vres:
            mix = jax.nn.sigmoid(v0 + _dot(lo_ref[rows, 2 * V7X_LANES:3 * V7X_LANES].astype(BF16), wv_ref[...]))
            v = v + (vfirst_ref[rows, :] - v) * mix
        yield
        z = -w_pre
        softplus = jnp.maximum(z, 0.0) + jnp.log1p(jnp.exp(-jnp.abs(z)))
        logw = -jnp.exp(-softplus - 0.5)
        kk = k * k_k
        kk = kk / jnp.maximum(jnp.sqrt(head_sum(kk * kk)), 1e-12)
        k = k * (1.0 + (a_sig - 1.0) * k_a)
        b_vec = kk * a_sig
        p["r"], p["k"], p["v"] = r, k, v
        yield
        tri = tri_ref[...]
        lh, lm, ll = _split3(logw)
        cum = _dot(tri, lh) + _dot(tri, lm) + _dot(tri, ll)
        tot = cum[c - 1:c, :]
        p["g_tot"] = jnp.exp(tot)
        yield
        e_inv = jnp.exp(-cum)
        e_hat = jnp.exp(tot - cum)
        stack = lambda x: [_stack_heads(x[:, sl], lane_is_head0).astype(BF16) for sl in lanes]
        p["a_s"] = stack(-kk * jnp.exp(cum - logw))
        yield
        p["r_s"] = stack(r * jnp.exp(cum))
        yield
        p["b_s"] = stack(b_vec * e_inv)
        yield
        p["k_s"] = stack(k * e_inv)
        yield
        p["v_s"] = stack(v)
        yield
        p["bh_s"] = stack(b_vec * e_hat)
        yield
        p["kh_s"] = stack(k * e_hat)

    def mix(rows, p):
        a_s, r_s, b_s, k_s, v_s = p["a_s"], p["r_s"], p["b_s"], p["k_s"], p["v_s"]
        zz = [_dot_nt(jnp.concatenate([a_s[s], r_s[s]], axis=0), jnp.concatenate([b_s[s], k_s[s]], axis=0))
              for s in slabs]
        yield
        a_ab = [jnp.where(strict, z[0:n2, 0:n2], 0.0) for z in zz]
        a_ak = [jnp.where(strict, z[0:n2, n2:2 * n2], 0.0).astype(BF16) for z in zz]
        m_rb = [jnp.where(incl, z[n2:2 * n2, 0:n2], 0.0).astype(BF16) for z in zz]
        m_rk = [jnp.where(incl, z[n2:2 * n2, n2:2 * n2], 0.0).astype(BF16) for z in zz]
        pb = [a.astype(BF16) for a in a_ab]
        t_inv = [eye + a for a in a_ab]
        pw = [_dot(q, q) for q in pb]
        yield
        akv = [_dot(a_ak[s], v_s[s]).astype(BF16) for s in slabs]
        sb = [state_ref[s].astype(BF16) for s in slabs]
        y_rs = [_dot_nt(r_s[s], sb[s]) + _dot(m_rk[s], v_s[s]) for s in slabs]
        yield
        steps = CHUNK.bit_length() - 2
        for it in range(steps):
            pb = [q.astype(BF16) for q in pw]
            if it + 1 < steps:
                both = [_dot(pb[s], jnp.concatenate([t_inv[s].astype(BF16), pb[s]], axis=1)) for s in slabs]
                t_inv = [t_inv[s] + both[s][:, 0:n2] for s in slabs]
                pw = [b2[:, n2:2 * n2] for b2 in both]
            else:
                t_inv = [t_inv[s] + _dot(pb[s], t_inv[s].astype(BF16)) for s in slabs]
            yield
        wu = [_dot(t_inv[s].astype(BF16), jnp.concatenate([a_s[s], akv[s]], axis=1)) for s in slabs]
        yield
        ub = [(_dot_nt(wu[s][:, 0:V7X_LANES].astype(BF16), sb[s]) + wu[s][:, V7X_LANES:2 * V7X_LANES]).astype(BF16)
              for s in slabs]
        yield
        for s in slabs:
            y_s = y_rs[s] + _dot(m_rb[s], ub[s])
            y_ref[rows, lanes[s]] = y_s[0:c, :] + y_s[c:2 * c, :]
        yield
        for s in slabs:
            state_ref[s] = state_ref[s] * p["g_tot"][:, lanes[s]] + _dot_tn(
                jnp.concatenate([ub[s], v_s[s]], axis=0), jnp.concatenate([p["bh_s"][s], p["kh_s"][s]], axis=0))

    def post(rows, p):
        y = y_ref[rows, :]
        inv_n = 1.0 / HEAD_SIZE
        mu = head_sum(y) * inv_n
        yield
        yc = y - mu
        var = head_sum(yc * yc) * inv_n
        yield
        yn = yc * lax.rsqrt(var + GN_EPS) * gn_gain + gn_bias
        bonus = head_sum(p["r"] * p["k"] * r_k) * p["v"]
        out_ref[rows, :] = (yn + bonus) * p["gate"]

    def drive(main, fillers):
        fillers = [f for f in fillers if f is not None]
        for _ in main:
            for f in fillers:
                next(f, None)
        for f in fillers:
            for _ in f:
                pass

    row_slices = [slice(i * c, (i + 1) * c) for i in range(CHUNKS_PER_STEP)]
    params = [dict() for _ in row_slices]
    drive(prep(row_slices[0], params[0]), [])
    for i, rows in enumerate(row_slices):
        nxt = prep(row_slices[i + 1], params[i + 1]) if i + 1 < len(row_slices) else None
        prv = post(row_slices[i - 1], params[i - 1]) if i > 0 else None
        drive(mix(rows, params[i]), [nxt, prv])
    drive(post(row_slices[-1], params[-1]), [])


def _rwkv_time_mix(rkv, lo, v_first_src, pvec, wd, wa, wg, wv, tri, ones_bd, *, bsz, seq, d):
    has_vres = v_first_src is not None
    step_rows = CHUNK * CHUNKS_PER_STEP
    steps = seq // step_rows
    n_slab = d // V7X_LANES
    row_map = lambda b, t: (b * steps + t, 0)
    in_specs = [pl.BlockSpec((step_rows, 3 * d), row_map), pl.BlockSpec((step_rows, LO_WIDTH), row_map)]
    args = [rkv, lo]
    if has_vres:
        in_specs.append(pl.BlockSpec((step_rows, d), lambda b, t: (b * steps + t, 2)))
        args.append(v_first_src)
    consts = [pvec, wd, wa, wg] + ([wv] if has_vres else []) + [tri, ones_bd]
    in_specs += [_const_spec(a.shape) for a in consts]
    args += consts
    return pl.pallas_call(
        functools.partial(_rwkv_kernel, d=d, has_vres=has_vres),
        grid=(bsz, steps),
        in_specs=in_specs,
        out_specs=pl.BlockSpec((step_rows, d), row_map),
        out_shape=jax.ShapeDtypeStruct((bsz * seq, d), F32),
        scratch_shapes=[pltpu.VMEM((n_slab, V7X_LANES, V7X_LANES), F32), pltpu.VMEM((step_rows, d), F32)],
        compiler_params=pltpu.CompilerParams(dimension_semantics=("arbitrary", "arbitrary"),
                                             vmem_limit_bytes=VMEM_LIMIT),
        name="rwkv7_time_mix",
    )(*args)


def _mix_mlp_kernel(ya_ref, z_ref, gate_ref, x_ref, mod_ref, cvec_ref, wout_ref, w1_ref, w2_ref, out_ref,
                    *, d, final_norm):
    tm = x_ref.shape[0]
    ln_gain, ln_bias = cvec_ref[0:1, :], cvec_ref[1:2, :]
    ffn_gain, final_gain = cvec_ref[2:3, :], cvec_ref[3:4, :]
    mod = mod_ref[0]
    gt_m = mod[:, 2 * d:3 * d]
    sh_f, sc_f, gt_f = mod[:, 3 * d:4 * d], mod[:, 4 * d:5 * d], mod[:, 5 * d:6 * d]

    m_a = (ya_ref[...] * gate_ref[:, 0:d]).astype(BF16)
    o_a = _dot(m_a, wout_ref[0:d, :])
    z = z_ref[...]
    mu = jnp.mean(z, axis=-1, keepdims=True)
    zc = z - mu
    var = jnp.mean(zc * zc, axis=-1, keepdims=True)
    zn = zc * lax.rsqrt(var + LN_EPS) * ln_gain + ln_bias
    y_b = zn * jax.nn.sigmoid(zn)
    m_b = (y_b * gate_ref[:, d:2 * d]).astype(BF16)
    x1 = x_ref[...] + gt_m * (o_a + _dot(m_b, wout_ref[d:2 * d, :]))

    ms = jnp.mean(x1 * x1, axis=-1, keepdims=True)
    h = ((x1 * lax.rsqrt(ms + RMS_EPS)) * ffn_gain * (1.0 + sc_f) + sh_f).astype(BF16)
    d_ff = w1_ref.shape[1]
    acc = jnp.zeros((tm, d), F32)
    for j in range(d_ff // d):
        hid = jnp.maximum(_dot(h, w1_ref[:, j * d:(j + 1) * d]), 0.0)
        acc = acc + _dot((hid * hid).astype(BF16), w2_ref[j * d:(j + 1) * d, :])
    x2 = x1 + gt_f * acc
    if final_norm:
        ms2 = jnp.mean(x2 * x2, axis=-1, keepdims=True)
        x2 = (x2 * lax.rsqrt(ms2 + RMS_EPS)) * final_gain
    out_ref[...] = x2


def _mix_mlp(ya, z, gates, x2, mod_l, cvec, wout, w1, w2, *, seq, d, final_norm):
    rows = x2.shape[0]
    tm = min(ROW_TILE, seq)
    tiles_per_seq = seq // tm
    row_spec = lambda width: pl.BlockSpec((tm, width), lambda i: (i, 0))
    return pl.pallas_call(
        functools.partial(_mix_mlp_kernel, d=d, final_norm=final_norm),
        grid=(rows // tm,),
        in_specs=[
            row_spec(d), row_spec(d), row_spec(2 * d), row_spec(d),
            pl.BlockSpec((1, 1, mod_l.shape[-1]), lambda i: (i // tiles_per_seq, 0, 0)),
            _const_spec(cvec.shape), _const_spec(wout.shape), _const_spec(w1.shape), _const_spec(w2.shape),
        ],
        out_specs=row_spec(d),
        out_shape=jax.ShapeDtypeStruct((rows, d), F32),
        compiler_params=pltpu.CompilerParams(dimension_semantics=("arbitrary",), vmem_limit_bytes=VMEM_LIMIT),
        name="merge_mlp",
    )(ya, z, gates, x2, mod_l, cvec, wout, w1, w2)


def _pad_rows(w, rows, row0=0):
    out = jnp.zeros((rows, w.shape[1]), w.dtype)
    return lax.dynamic_update_slice(out, w, (row0, 0))


def kernel(x, c, norm_mix_gain, norm_ffn_gain, ada_w, ada_b, w_in, w_in_vres, mu_shift, mu_vres, w0, w_decay_up, a0, w_aaa_up, w_gate_up, k_k, k_a, r_k, gn_gain, gn_bias, v0, w_vres_up, conv_w, conv_b, conv_ln_gain, conv_ln_bias, w_out, w_ff_in, w_ff_out, final_gain):
    bsz, seq, d = x.shape
    depth = w_in.shape[0]
    n_rkv = 3 * d
    n_lo = DECAY_RANK + AAA_RANK + GATE_RANK
    n_shift = n_rkv + n_lo
    n_gate = 2 * d
    assert d % V7X_MXU_WIDTH == 0 and seq % (CHUNK * CHUNKS_PER_STEP) == 0 and seq % min(ROW_TILE, seq) == 0
    assert w_in.shape[2] == n_shift + 2 * d + n_gate and ada_w.shape[2] % MOD_COL_TILE == 0

    mod = _modulation(c, ada_w, ada_b)
    tri = jnp.tril(jnp.ones((CHUNK, CHUNK), BF16))
    lane = jnp.arange(V7X_MXU_WIDTH) // HEAD_SIZE
    ones_bd = (lane[:, None] == lane[None, :]).astype(BF16)

    x2 = x.reshape(bsz * seq, d)
    v_first_src = None
    for l in range(depth):
        has_vres = l > 0
        lo_pad = LO_WIDTH - n_lo
        if has_vres:
            w_tail = jnp.pad(w_in_vres[l - 1], ((0, 0), (0, lo_pad - VRES_RANK)))
            mu_tail = jnp.pad(mu_vres[l - 1], (0, lo_pad - VRES_RANK))
        else:
            w_tail = jnp.zeros((d, lo_pad), F32)
            mu_tail = jnp.zeros((lo_pad,), F32)
        w_all = jnp.concatenate([w_in[l][:, :n_shift], w_tail, w_in[l][:, n_shift:]], axis=1).astype(BF16)
        mu_all = jnp.concatenate([mu_shift[l], mu_tail]).reshape(1, n_shift + lo_pad)
        mod_l = mod[l].reshape(bsz, 1, 6 * d)

        rkv, lo, z, gates = _in_projection(x2, mod_l, norm_mix_gain[l].reshape(1, d), mu_all,
                                           _pad_rows(conv_w[l], CONV_HALO), conv_b[l].reshape(1, d), w_all,
                                           seq=seq, d=d, n_rkv=n_rkv, n_gate=n_gate)

        pvec = jnp.stack([w0[l], a0[l], k_k[l], k_a[l], r_k[l].reshape(d), gn_gain[l], gn_bias[l],
                          v0[l - 1] if has_vres else jnp.zeros((d,), F32)])
        wd = _pad_rows(w_decay_up[l], V7X_LANES, 0).astype(BF16)
        wa = _pad_rows(w_aaa_up[l], V7X_LANES, DECAY_RANK).astype(BF16)
        wg = w_gate_up[l].astype(BF16)
        wv = _pad_rows(w_vres_up[l - 1], V7X_LANES, 0).astype(BF16) if has_vres else None
        ya = _rwkv_time_mix(rkv, lo, v_first_src, pvec, wd, wa, wg, wv, tri, ones_bd, bsz=bsz, seq=seq, d=d)
        if l == 0:
            v_first_src = rkv

        cvec = jnp.stack([conv_ln_gain[l], conv_ln_bias[l], norm_ffn_gain[l], final_gain] + [jnp.zeros((d,), F32)] * 4)
        x2 = _mix_mlp(ya, z, gates, x2, mod_l, cvec, w_out[l].astype(BF16), w_ff_in[l].astype(BF16),
                      w_ff_out[l].astype(BF16), seq=seq, d=d, final_norm=(l == depth - 1))
    return x2.reshape(bsz, seq, d)
```

```python
import functools

import jax
import jax.numpy as jnp
from jax import lax
from jax.experimental import pallas as pl
from jax.experimental.pallas import tpu as pltpu

F32 = jnp.float32
BF16 = jnp.bfloat16

V7X_LANES = 128
V7X_VMEM_BYTES = 64 * 1024 * 1024
V7X_MXU_WIDTH = 256

HEAD_SIZE = 64
HEADS_PER_SLAB = V7X_LANES // HEAD_SIZE
CHUNK = 64
CHUNKS_PER_STEP = 4
SEQS_PER_STEP = 2
DECAY_RANK = 64
AAA_RANK = 64
GATE_RANK = 128
VRES_RANK = 32
LO_WIDTH = 3 * V7X_LANES
CONV_WIDTH = 31
CONV_HALO = 32
CONV_ROWS = 64
CONV_GROUPS = 4

RMS_EPS = 1e-6
LN_EPS = 1e-5
GN_EPS = 64e-5

ROW_TILE = 256
MOD_COL_TILE = 1536
VMEM_LIMIT = 56 * 1024 * 1024


def _dot(a, b):
    return jnp.dot(a, b, preferred_element_type=F32)


def _dot_nt(a, b):
    return lax.dot_general(a, b, (((1,), (1,)), ((), ())), preferred_element_type=F32)


def _dot_tn(a, b):
    return lax.dot_general(a, b, (((0,), (0,)), ((), ())), preferred_element_type=F32)


def _split3(x):
    hi = x.astype(BF16)
    r1 = x - hi.astype(F32)
    mid = r1.astype(BF16)
    lo = (r1 - mid.astype(F32)).astype(BF16)
    return hi, mid, lo


def _const_spec(shape):
    nd = len(shape)
    return pl.BlockSpec(shape, lambda *_: (0,) * nd, pipeline_mode=pl.Buffered(1))


def _mod_kernel(c_ref, w_ref, b_ref, o_ref):
    c = c_ref[...]
    ca = c * jax.nn.sigmoid(c)
    ch, cm, _ = _split3(ca)
    wh, wm, _ = _split3(w_ref[0])
    o_ref[0] = _dot(ch, wh) + _dot(ch, wm) + _dot(cm, wh) + b_ref[0]


def _modulation(c, ada_w, ada_b):
    depth, d, n = ada_w.shape
    bsz = c.shape[0]
    return pl.pallas_call(
        _mod_kernel,
        grid=(depth, n // MOD_COL_TILE),
        in_specs=[
            pl.BlockSpec((bsz, d), lambda l, j: (0, 0)),
            pl.BlockSpec((1, d, MOD_COL_TILE), lambda l, j: (l, 0, j)),
            pl.BlockSpec((1, 1, MOD_COL_TILE), lambda l, j: (l, 0, j)),
        ],
        out_specs=pl.BlockSpec((1, bsz, MOD_COL_TILE), lambda l, j: (l, 0, j)),
        out_shape=jax.ShapeDtypeStruct((depth, bsz, n), F32),
        compiler_params=pltpu.CompilerParams(dimension_semantics=("arbitrary", "arbitrary")),
        name="adaln_modulation",
    )(c, ada_w, ada_b.reshape(depth, 1, n))


def _causal_conv_slab(hist_ref, cw_ref, cb_ref, out_ref, sl, tm):
    base = CONV_HALO - (CONV_WIDTH - 1)
    n = CONV_ROWS + CONV_HALO
    for g in range(tm // CONV_ROWS):
        r0 = g * CONV_ROWS
        x = hist_ref[r0:r0 + n, sl]
        acc = jnp.broadcast_to(cb_ref[:, sl], (CONV_ROWS, V7X_LANES))
        for res in range(8):
            xr = x if res == 0 else pltpu.roll(x, n - res, 0)
            for j in range(CONV_WIDTH):
                if (base + j) % 8 == res:
                    q0 = base + j - res
                    acc = acc + cw_ref[j:j + 1, sl] * xr[q0:q0 + CONV_ROWS, :]
        out_ref[r0:r0 + CONV_ROWS, sl] = acc


def _inproj_kernel(x_ref, mod_ref, gain_ref, mu_ref, cw_ref, cb_ref, w_ref, rkv_ref, lo_ref, z_ref, gate_ref,
                   carry_ref, hist_ref, glu_ref, *, tiles_per_seq, n_tiles, d):
    tm = x_ref.shape[0]
    step = pl.program_id(0)
    n_shift = rkv_ref.shape[1] + lo_ref.shape[1]
    n_rkv = rkv_ref.shape[1]
    n_slab = d // V7X_LANES
    assert n_rkv == 3 * d and gate_ref.shape[1] == 2 * d and n_slab % CONV_GROUPS == 0 and CONV_GROUPS == 4
    per = n_slab // CONV_GROUPS
    part = d // CONV_GROUPS
    projecting = step < n_tiles
    first_row = lax.broadcasted_iota(jnp.int32, (tm, 1), 0) == 0

    @pl.when(step == 0)
    def _():
        hist_ref[...] = jnp.zeros_like(hist_ref)

    @pl.when(step % tiles_per_seq == 0)
    def _():
        carry_ref[...] = jnp.zeros_like(carry_ref)

    x = x_ref[...]
    mod = mod_ref[0]
    shift, scale = mod[:, 0:d], mod[:, d:2 * d]
    ms = jnp.mean(x * x, axis=-1, keepdims=True)
    h = (x * lax.rsqrt(ms + RMS_EPS)) * gain_ref[...] * (1.0 + scale) + shift
    hb = h.astype(BF16)

    def shifted(col0, width, out_ref, out_col0):
        p = _dot(hb, w_ref[:, col0:col0 + width])
        prev = jnp.where(first_row, carry_ref[0:1, col0:col0 + width], pltpu.roll(p, 1, 0))
        carry_ref[0:1, col0:col0 + width] = p[tm - 1:tm, :]
        out_ref[:, out_col0:out_col0 + width] = p + (prev - p) * mu_ref[:, col0:col0 + width]

    def gate(col0, width):
        g0 = n_shift + 2 * d + col0
        gate_ref[:, col0:col0 + width] = jax.nn.sigmoid(_dot(hb, w_ref[:, g0:g0 + width]))

    def glu(col0, width):
        u1 = _dot(hb, w_ref[:, n_shift + col0:n_shift + col0 + width])
        u2 = _dot(hb, w_ref[:, n_shift + d + col0:n_shift + d + col0 + width])
        glu_ref[:, col0:col0 + width] = u1 * jax.nn.sigmoid(u2)

    def conv(slabs):
        for s in slabs:
            _causal_conv_slab(hist_ref, cw_ref, cb_ref, z_ref, slice(s * V7X_LANES, (s + 1) * V7X_LANES), tm)

    for g in range(CONV_GROUPS):
        @pl.when(projecting)
        def _(g=g):
            glu(g * part, part)
            if g < 3:
                shifted(g * d, d, rkv_ref, g * d)
            else:
                shifted(n_rkv, lo_ref.shape[1], lo_ref, 0)
                gate(0, d)
            gate(d + g * part, part)
            conv(range(g * per, (g + 1) * per))

    @pl.when(step == n_tiles)
    def _():
        conv(range(n_slab))

    @pl.when(projecting)
    def _():
        halo = jnp.where(step % tiles_per_seq == 0, 0.0, hist_ref[tm:tm + CONV_HALO, :])
        hist_ref[0:CONV_HALO, :] = halo
        hist_ref[CONV_HALO:CONV_HALO + tm, :] = glu_ref[...]


def _in_projection(x2, mod_l, gain, mu_all, cw, cb, w_all, *, seq, d, n_rkv, n_gate):
    rows = x2.shape[0]
    tm = min(ROW_TILE, seq)
    tiles_per_seq = seq // tm
    n_tiles = rows // tm
    n_all = w_all.shape[1]
    cur = lambda i: jnp.minimum(i, n_tiles - 1)
    row_spec = lambda width: pl.BlockSpec((tm, width), lambda i: (cur(i), 0))
    return pl.pallas_call(
        functools.partial(_inproj_kernel, tiles_per_seq=tiles_per_seq, n_tiles=n_tiles, d=d),
        grid=(n_tiles + 1,),
        in_specs=[
            row_spec(d),
            pl.BlockSpec((1, 1, mod_l.shape[-1]), lambda i: (cur(i) // tiles_per_seq, 0, 0)),
            _const_spec((1, d)),
            _const_spec((1, n_rkv + LO_WIDTH)),
            _const_spec(cw.shape),
            _const_spec((1, d)),
            _const_spec((d, n_all)),
        ],
        out_specs=[row_spec(n_rkv), row_spec(LO_WIDTH),
                   pl.BlockSpec((tm, d), lambda i: (jnp.maximum(i - 1, 0), 0)), row_spec(n_gate)],
        out_shape=[
            jax.ShapeDtypeStruct((rows, n_rkv), F32),
            jax.ShapeDtypeStruct((rows, LO_WIDTH), F32),
            jax.ShapeDtypeStruct((rows, d), F32),
            jax.ShapeDtypeStruct((rows, n_gate), F32),
        ],
        scratch_shapes=[pltpu.VMEM((8, n_rkv + LO_WIDTH), F32), pltpu.VMEM((CONV_HALO + tm, d), F32),
                        pltpu.VMEM((tm, d), F32)],
        compiler_params=pltpu.CompilerParams(dimension_semantics=("arbitrary",), vmem_limit_bytes=VMEM_LIMIT),
        name="in_projection",
    )(x2, mod_l, gain, mu_all, cw, cb, w_all)


def _stack_heads(x, lane_is_head0):
    zero = jnp.zeros_like(x)
    return jnp.concatenate([jnp.where(lane_is_head0, x, zero), jnp.where(lane_is_head0, zero, x)], axis=0)


def _rwkv_kernel(*refs, d, has_vres):
    if has_vres:
        (rkv_ref, lo_ref, vfirst_ref, pvec_ref, wd_ref, wa_ref, wg_ref, wv_ref, tri_ref, ones_ref,
         out_ref, state_ref, y_ref) = refs
    else:
        (rkv_ref, lo_ref, pvec_ref, wd_ref, wa_ref, wg_ref, tri_ref, ones_ref,
         out_ref, state_ref, y_ref) = refs
    c = CHUNK
    n2 = 2 * c
    n_slab = d // V7X_LANES
    slabs = range(n_slab)
    lanes = [slice(s * V7X_LANES, (s + 1) * V7X_LANES) for s in slabs]

    @pl.when(pl.program_id(1) == 0)
    def _():
        state_ref[...] = jnp.zeros_like(state_ref)

    w0, a0, k_k, k_a = (pvec_ref[i:i + 1, :] for i in range(4))
    r_k, gn_gain, gn_bias, v0 = (pvec_ref[i:i + 1, :] for i in range(4, 8))
    lane_is_head0 = lax.broadcasted_iota(jnp.int32, (c, V7X_LANES), 1) < HEAD_SIZE
    ri = lax.broadcasted_iota(jnp.int32, (n2, n2), 0) & (c - 1)
    ci = lax.broadcasted_iota(jnp.int32, (n2, n2), 1) & (c - 1)
    strict = ri > ci
    incl = ri >= ci
    eye = (lax.broadcasted_iota(jnp.int32, (n2, n2), 0) ==
           lax.broadcasted_iota(jnp.int32, (n2, n2), 1)).astype(F32)

    def head_sum(x):
        wide = ones_ref.shape[0]
        xs = jnp.concatenate([x[:, j:j + wide] for j in range(0, d, wide)], axis=0)
        ss = _dot(xs.astype(BF16), ones_ref[...])
        return jnp.concatenate([ss[j * c:(j + 1) * c, :] for j in range(d // wide)], axis=1)

    def prep(q, rows, p):
        r = rkv_ref[q, rows, 0:d]
        k = rkv_ref[q, rows, d:2 * d]
        v = rkv_ref[q, rows, 2 * d:3 * d]
        lo_wa = lo_ref[q, rows, 0:V7X_LANES]
        w_pre = w0 + _dot(jnp.tanh(lo_wa).astype(BF16), wd_ref[...])
        a_sig = jax.nn.sigmoid(a0 + _dot(lo_wa.astype(BF16), wa_ref[...]))
        p["gate"] = _dot(jax.nn.sigmoid(lo_ref[q, rows, V7X_LANES:2 * V7X_LANES]).astype(BF16), wg_ref[...])
        if has_vres:
            mix = jax.nn.sigmoid(v0 + _dot(lo_ref[q, rows, 2 * V7X_LANES:3 * V7X_LANES].astype(BF16), wv_ref[...]))
            v = v + (vfirst_ref[q, rows, :] - v) * mix
        yield
        z = -w_pre
        softplus = jnp.maximum(z, 0.0) + jnp.log1p(jnp.exp(-jnp.abs(z)))
        logw = -jnp.exp(-softplus - 0.5)
        kk = k * k_k
        kk = kk * jnp.minimum(lax.rsqrt(head_sum(kk * kk)), 1e12)
        k = k * (1.0 + (a_sig - 1.0) * k_a)
        b_vec = kk * a_sig
        p["r"], p["k"], p["v"] = r, k, v
        yield
        tri = tri_ref[...]
        lh, lm, ll = _split3(logw)
        cum = _dot(tri, lh) + _dot(tri, lm) + _dot(tri, ll)
        tot = cum[c - 1:c, :]
        g_tot = jnp.exp(tot)
        p["g_tot"] = g_tot
        yield
        e_inv = jnp.exp(-cum)
        stack = lambda x: [_stack_heads(x.astype(BF16)[:, sl], lane_is_head0) for sl in lanes]
        p["a_s"] = stack(-kk * jnp.exp(cum - logw))
        yield
        p["r_s"] = stack(r * jnp.exp(cum))
        yield
        b_t = b_vec * e_inv
        p["b_s"] = stack(b_t)
        yield
        k_t = k * e_inv
        p["k_s"] = stack(k_t)
        yield
        p["v_s"] = stack(v)
        yield
        p["bh_s"] = stack(b_t * g_tot)
        yield
        p["kh_s"] = stack(k_t * g_tot)

    def mix(q, rows, p):
        a_s, r_s, b_s, k_s, v_s = p["a_s"], p["r_s"], p["b_s"], p["k_s"], p["v_s"]
        zz = [_dot_nt(jnp.concatenate([a_s[s], r_s[s]], axis=0), jnp.concatenate([b_s[s], k_s[s]], axis=0))
              for s in slabs]
        yield
        a_ab = [jnp.where(strict, z[0:n2, 0:n2], 0.0) for z in zz]
        a_ak = [jnp.where(strict, z[0:n2, n2:2 * n2], 0.0).astype(BF16) for z in zz]
        m_rb = [jnp.where(incl, z[n2:2 * n2, 0:n2], 0.0).astype(BF16) for z in zz]
        m_rk = [jnp.where(incl, z[n2:2 * n2, n2:2 * n2], 0.0).astype(BF16) for z in zz]
        pb = [a.astype(BF16) for a in a_ab]
        t_inv = [eye + a for a in a_ab]
        pw = [_dot(m, m) for m in pb]
        yield
        akv = [_dot(a_ak[s], v_s[s]).astype(BF16) for s in slabs]
        sb = [state_ref[q, s].astype(BF16) for s in slabs]
        y_rs = [_dot_nt(r_s[s], sb[s]) + _dot(m_rk[s], v_s[s]) for s in slabs]
        yield
        steps = CHUNK.bit_length() - 2
        for it in range(steps):
            pb = [m.astype(BF16) for m in pw]
            if it + 1 < steps:
                both = [_dot(pb[s], jnp.concatenate([t_inv[s].astype(BF16), pb[s]], axis=1)) for s in slabs]
                t_inv = [t_inv[s] + both[s][:, 0:n2] for s in slabs]
                pw = [b2[:, n2:2 * n2] for b2 in both]
            else:
                t_inv = [t_inv[s] + _dot(pb[s], t_inv[s].astype(BF16)) for s in slabs]
            yield
        wu = [_dot(t_inv[s].astype(BF16), jnp.concatenate([a_s[s], akv[s]], axis=1)) for s in slabs]
        yield
        ub = [(_dot_nt(wu[s][:, 0:V7X_LANES].astype(BF16), sb[s]) + wu[s][:, V7X_LANES:2 * V7X_LANES]).astype(BF16)
              for s in slabs]
        yield
        for s in slabs:
            y_s = y_rs[s] + _dot(m_rb[s], ub[s])
            y_ref[q, rows, lanes[s]] = y_s[0:c, :] + y_s[c:2 * c, :]
        yield
        for s in slabs:
            state_ref[q, s] = state_ref[q, s] * p["g_tot"][:, lanes[s]] + _dot_tn(
                jnp.concatenate([ub[s], v_s[s]], axis=0), jnp.concatenate([p["bh_s"][s], p["kh_s"][s]], axis=0))

    def post(q, rows, p):
        y = y_ref[q, rows, :]
        inv_n = 1.0 / HEAD_SIZE
        mu = head_sum(y) * inv_n
        yield
        yc = y - mu
        var = head_sum(yc * yc) * inv_n
        yield
        yn = yc * lax.rsqrt(var + GN_EPS) * gn_gain + gn_bias
        bonus = head_sum(p["r"] * p["k"] * r_k) * p["v"]
        out_ref[q, rows, :] = (yn + bonus) * p["gate"]

    def lockstep(gens):
        live = list(gens)
        while live:
            live = [g for g in live if next(g, StopIteration) is not StopIteration]
            if live:
                yield

    def drive(main, fillers):
        for _ in main:
            for f in fillers:
                next(f, None)
        for f in fillers:
            for _ in f:
                pass

    streams = range(SEQS_PER_STEP)
    row_slices = [slice(i * c, (i + 1) * c) for i in range(CHUNKS_PER_STEP)]
    params = [[dict() for _ in row_slices] for _ in streams]
    drive(lockstep([prep(q, row_slices[0], params[q][0]) for q in streams]), [])
    for i, rows in enumerate(row_slices):
        fillers = []
        if i + 1 < len(row_slices):
            fillers += [prep(q, row_slices[i + 1], params[q][i + 1]) for q in streams]
        if i > 0:
            fillers += [post(q, row_slices[i - 1], params[q][i - 1]) for q in streams]
        drive(lockstep([mix(q, rows, params[q][i]) for q in streams]), fillers)
    drive(lockstep([post(q, row_slices[-1], params[q][-1]) for q in streams]), [])


def _rwkv_time_mix(rkv, lo, v_first_src, pvec, wd, wa, wg, wv, tri, ones_bd, *, bsz, seq, d):
    has_vres = v_first_src is not None
    step_rows = CHUNK * CHUNKS_PER_STEP
    steps = seq // step_rows
    n_slab = d // V7X_LANES
    nq = SEQS_PER_STEP
    blk = lambda width, col: pl.BlockSpec((nq, step_rows, width), lambda b, t: (b, t, col))
    in_specs = [blk(3 * d, 0), blk(LO_WIDTH, 0)]
    args = [rkv.reshape(bsz, seq, 3 * d), lo.reshape(bsz, seq, LO_WIDTH)]
    if has_vres:
        in_specs.append(blk(d, 2))
        args.append(v_first_src.reshape(bsz, seq, 3 * d))
    consts = [pvec, wd, wa, wg] + ([wv] if has_vres else []) + [tri, ones_bd]
    in_specs += [_const_spec(a.shape) for a in consts]
    args += consts
    out = pl.pallas_call(
        functools.partial(_rwkv_kernel, d=d, has_vres=has_vres),
        grid=(bsz // nq, steps),
        in_specs=in_specs,
        out_specs=blk(d, 0),
        out_shape=jax.ShapeDtypeStruct((bsz, seq, d), F32),
        scratch_shapes=[pltpu.VMEM((nq, n_slab, V7X_LANES, V7X_LANES), F32), pltpu.VMEM((nq, step_rows, d), F32)],
        compiler_params=pltpu.CompilerParams(dimension_semantics=("arbitrary", "arbitrary"),
                                             vmem_limit_bytes=VMEM_LIMIT),
        name="rwkv7_time_mix",
    )(*args)
    return out.reshape(bsz * seq, d)


def _mix_mlp_kernel(ya_ref, z_ref, gate_ref, x_ref, mod_ref, cvec_ref, wout_ref, w1_ref, w2_ref, out_ref,
                    *, d, final_norm):
    tm = x_ref.shape[0]
    ln_gain, ln_bias = cvec_ref[0:1, :], cvec_ref[1:2, :]
    ffn_gain, final_gain = cvec_ref[2:3, :], cvec_ref[3:4, :]
    mod = mod_ref[0]
    gt_m = mod[:, 2 * d:3 * d]
    sh_f, sc_f, gt_f = mod[:, 3 * d:4 * d], mod[:, 4 * d:5 * d], mod[:, 5 * d:6 * d]

    m_a = (ya_ref[...] * gate_ref[:, 0:d]).astype(BF16)
    o_a = _dot(m_a, wout_ref[0:d, :])
    z = z_ref[...]
    mu = jnp.mean(z, axis=-1, keepdims=True)
    zc = z - mu
    var = jnp.mean(zc * zc, axis=-1, keepdims=True)
    zn = zc * lax.rsqrt(var + LN_EPS) * ln_gain + ln_bias
    y_b = zn * jax.nn.sigmoid(zn)
    m_b = (y_b * gate_ref[:, d:2 * d]).astype(BF16)
    x1 = x_ref[...] + gt_m * (o_a + _dot(m_b, wout_ref[d:2 * d, :]))

    ms = jnp.mean(x1 * x1, axis=-1, keepdims=True)
    h = ((x1 * lax.rsqrt(ms + RMS_EPS)) * ffn_gain * (1.0 + sc_f) + sh_f).astype(BF16)
    d_ff = w1_ref.shape[1]
    acc = jnp.zeros((tm, d), F32)
    for j in range(d_ff // d):
        hid = jnp.maximum(_dot(h, w1_ref[:, j * d:(j + 1) * d]), 0.0)
        acc = acc + _dot((hid * hid).astype(BF16), w2_ref[j * d:(j + 1) * d, :])
    x2 = x1 + gt_f * acc
    if final_norm:
        ms2 = jnp.mean(x2 * x2, axis=-1, keepdims=True)
        x2 = (x2 * lax.rsqrt(ms2 + RMS_EPS)) * final_gain
    out_ref[...] = x2


def _mix_mlp(ya, z, gates, x2, mod_l, cvec, wout, w1, w2, *, seq, d, final_norm):
    rows = x2.shape[0]
    tm = min(ROW_TILE, seq)
    tiles_per_seq = seq // tm
    row_spec = lambda width: pl.BlockSpec((tm, width), lambda i: (i, 0))
    return pl.pallas_call(
        functools.partial(_mix_mlp_kernel, d=d, final_norm=final_norm),
        grid=(rows // tm,),
        in_specs=[
            row_spec(d), row_spec(d), row_spec(2 * d), row_spec(d),
            pl.BlockSpec((1, 1, mod_l.shape[-1]), lambda i: (i // tiles_per_seq, 0, 0)),
            _const_spec(cvec.shape), _const_spec(wout.shape), _const_spec(w1.shape), _const_spec(w2.shape),
        ],
        out_specs=row_spec(d),
        out_shape=jax.ShapeDtypeStruct((rows, d), F32),
        compiler_params=pltpu.CompilerParams(dimension_semantics=("arbitrary",), vmem_limit_bytes=VMEM_LIMIT),
        name="merge_mlp",
    )(ya, z, gates, x2, mod_l, cvec, wout, w1, w2)


def _pad_rows(w, rows, row0=0):
    out = jnp.zeros((rows, w.shape[1]), w.dtype)
    return lax.dynamic_update_slice(out, w, (row0, 0))


def kernel(x, c, norm_mix_gain, norm_ffn_gain, ada_w, ada_b, w_in, w_in_vres, mu_shift, mu_vres, w0, w_decay_up, a0, w_aaa_up, w_gate_up, k_k, k_a, r_k, gn_gain, gn_bias, v0, w_vres_up, conv_w, conv_b, conv_ln_gain, conv_ln_bias, w_out, w_ff_in, w_ff_out, final_gain):
    bsz, seq, d = x.shape
    depth = w_in.shape[0]
    n_rkv = 3 * d
    n_lo = DECAY_RANK + AAA_RANK + GATE_RANK
    n_shift = n_rkv + n_lo
    n_gate = 2 * d
    assert bsz % SEQS_PER_STEP == 0 and d % V7X_MXU_WIDTH == 0 and seq % (CHUNK * CHUNKS_PER_STEP) == 0 and seq % min(ROW_TILE, seq) == 0
    assert w_in.shape[2] == n_shift + 2 * d + n_gate and ada_w.shape[2] % MOD_COL_TILE == 0

    mod = _modulation(c, ada_w, ada_b)
    tri = jnp.tril(jnp.ones((CHUNK, CHUNK), BF16))
    lane = jnp.arange(V7X_MXU_WIDTH) // HEAD_SIZE
    ones_bd = (lane[:, None] == lane[None, :]).astype(BF16)

    x2 = x.reshape(bsz * seq, d)
    v_first_src = None
    for l in range(depth):
        has_vres = l > 0
        lo_pad = LO_WIDTH - n_lo
        if has_vres:
            w_tail = jnp.pad(w_in_vres[l - 1], ((0, 0), (0, lo_pad - VRES_RANK)))
            mu_tail = jnp.pad(mu_vres[l - 1], (0, lo_pad - VRES_RANK))
        else:
            w_tail = jnp.zeros((d, lo_pad), F32)
            mu_tail = jnp.zeros((lo_pad,), F32)
        w_all = jnp.concatenate([w_in[l][:, :n_shift], w_tail, w_in[l][:, n_shift:]], axis=1).astype(BF16)
        mu_all = jnp.concatenate([mu_shift[l], mu_tail]).reshape(1, n_shift + lo_pad)
        mod_l = mod[l].reshape(bsz, 1, 6 * d)

        rkv, lo, z, gates = _in_projection(x2, mod_l, norm_mix_gain[l].reshape(1, d), mu_all,
                                           _pad_rows(conv_w[l], CONV_HALO), conv_b[l].reshape(1, d), w_all,
                                           seq=seq, d=d, n_rkv=n_rkv, n_gate=n_gate)

        pvec = jnp.stack([w0[l], a0[l], k_k[l], k_a[l], r_k[l].reshape(d), gn_gain[l], gn_bias[l],
                          v0[l - 1] if has_vres else jnp.zeros((d,), F32)])
        wd = _pad_rows(w_decay_up[l], V7X_LANES, 0).astype(BF16)
        wa = _pad_rows(w_aaa_up[l], V7X_LANES, DECAY_RANK).astype(BF16)
        wg = w_gate_up[l].astype(BF16)
        wv = _pad_rows(w_vres_up[l - 1], V7X_LANES, 0).astype(BF16) if has_vres else None
        ya = _rwkv_time_mix(rkv, lo, v_first_src, pvec, wd, wa, wg, wv, tri, ones_bd, bsz=bsz, seq=seq, d=d)
        if l == 0:
            v_first_src = rkv

        cvec = jnp.stack([conv_ln_gain[l], conv_ln_bias[l], norm_ffn_gain[l], final_gain] + [jnp.zeros((d,), F32)] * 4)
        x2 = _mix_mlp(ya, z, gates, x2, mod_l, cvec, w_out[l].astype(BF16), w_ff_in[l].astype(BF16),
                      w_ff_out[l].astype(BF16), seq=seq, d=d, final_norm=(l == depth - 1))
    return x2.reshape(bsz, seq, d)
```

```python
import functools

import jax
import jax.numpy as jnp
from jax import lax
from jax.experimental import pallas as pl
from jax.experimental.pallas import tpu as pltpu

F32 = jnp.float32
BF16 = jnp.bfloat16

V7X_LANES = 128
V7X_VMEM_BYTES = 64 * 1024 * 1024
V7X_MXU_WIDTH = 256

HEAD_SIZE = 64
HEADS_PER_SLAB = V7X_LANES // HEAD_SIZE
CHUNK = 64
CHUNKS_PER_STEP = 4
SEQS_PER_STEP = 2
DECAY_RANK = 64
AAA_RANK = 64
GATE_RANK = 128
VRES_RANK = 32
LO_WIDTH = 3 * V7X_LANES
CONV_WIDTH = 31
CONV_HALO = 32
CONV_ROWS = 64
CONV_GROUPS = 4

RMS_EPS = 1e-6
LN_EPS = 1e-5
GN_EPS = 64e-5

ROW_TILE = 256
MLP_ROW_TILE = 512
MOD_COL_TILE = 1536
VMEM_LIMIT = 56 * 1024 * 1024


def _dot(a, b):
    return jnp.dot(a, b, preferred_element_type=F32)


def _dot_nt(a, b):
    return lax.dot_general(a, b, (((1,), (1,)), ((), ())), preferred_element_type=F32)


def _dot_tn(a, b):
    return lax.dot_general(a, b, (((0,), (0,)), ((), ())), preferred_element_type=F32)


def _split3(x):
    hi = x.astype(BF16)
    r1 = x - hi.astype(F32)
    mid = r1.astype(BF16)
    lo = (r1 - mid.astype(F32)).astype(BF16)
    return hi, mid, lo


def _const_spec(shape):
    nd = len(shape)
    return pl.BlockSpec(shape, lambda *_: (0,) * nd, pipeline_mode=pl.Buffered(1))


def _layer_spec(stacked_shape, layer):
    nd = len(stacked_shape) - 1
    return pl.BlockSpec((None,) + tuple(stacked_shape[1:]), lambda *_: (layer,) + (0,) * nd,
                        pipeline_mode=pl.Buffered(1))


def _mod_kernel(c_ref, w_ref, b_ref, o_ref):
    c = c_ref[...]
    ca = c * jax.nn.sigmoid(c)
    ch, cm, _ = _split3(ca)
    wh, wm, _ = _split3(w_ref[0])
    o_ref[0] = _dot(ch, wh) + _dot(ch, wm) + _dot(cm, wh) + b_ref[0]


def _modulation(c, ada_w, ada_b):
    depth, d, n = ada_w.shape
    bsz = c.shape[0]
    return pl.pallas_call(
        _mod_kernel,
        grid=(depth, n // MOD_COL_TILE),
        in_specs=[
            pl.BlockSpec((bsz, d), lambda l, j: (0, 0)),
            pl.BlockSpec((1, d, MOD_COL_TILE), lambda l, j: (l, 0, j)),
            pl.BlockSpec((1, 1, MOD_COL_TILE), lambda l, j: (l, 0, j)),
        ],
        out_specs=pl.BlockSpec((1, bsz, MOD_COL_TILE), lambda l, j: (l, 0, j)),
        out_shape=jax.ShapeDtypeStruct((depth, bsz, n), F32),
        compiler_params=pltpu.CompilerParams(dimension_semantics=("arbitrary", "arbitrary")),
        name="adaln_modulation",
    )(c, ada_w, ada_b.reshape(depth, 1, n))


def _causal_conv_slab(hist_ref, cw_ref, cb_ref, out_ref, sl, tm):
    base = CONV_HALO - (CONV_WIDTH - 1)
    n = CONV_ROWS + CONV_HALO
    for g in range(tm // CONV_ROWS):
        r0 = g * CONV_ROWS
        x = hist_ref[r0:r0 + n, sl]
        acc = jnp.broadcast_to(cb_ref[:, sl], (CONV_ROWS, V7X_LANES))
        for res in range(8):
            xr = x if res == 0 else pltpu.roll(x, n - res, 0)
            for j in range(CONV_WIDTH):
                if (base + j) % 8 == res:
                    q0 = base + j - res
                    acc = acc + cw_ref[j:j + 1, sl] * xr[q0:q0 + CONV_ROWS, :]
        out_ref[r0:r0 + CONV_ROWS, sl] = acc


def _inproj_kernel(x_ref, mod_ref, gain_ref, mu_ref, cw_ref, cb_ref, w_ref, rkv_ref, lo_ref, z_ref, gate_ref,
                   carry_ref, hist_ref, *, tiles_per_seq, d):
    tm = x_ref.shape[0]
    n_shift = rkv_ref.shape[1] + lo_ref.shape[1]

    @pl.when(pl.program_id(0) % tiles_per_seq == 0)
    def _():
        carry_ref[...] = jnp.zeros_like(carry_ref)
        hist_ref[0:CONV_HALO, :] = jnp.zeros((CONV_HALO, d), F32)

    x = x_ref[...]
    mod = mod_ref[0]
    shift, scale = mod[:, 0:d], mod[:, d:2 * d]
    ms = jnp.mean(x * x, axis=-1, keepdims=True)
    h = (x * lax.rsqrt(ms + RMS_EPS)) * gain_ref[...] * (1.0 + scale) + shift
    hb = h.astype(BF16)
    first_row = lax.broadcasted_iota(jnp.int32, (tm, 1), 0) == 0

    def shifted(col0, width, out_ref, out_col0):
        p = _dot(hb, w_ref[:, col0:col0 + width])
        prev = jnp.where(first_row, carry_ref[0:1, col0:col0 + width], pltpu.roll(p, 1, 0))
        carry_ref[0:1, col0:col0 + width] = p[tm - 1:tm, :]
        out_ref[:, out_col0:out_col0 + width] = p + (prev - p) * mu_ref[:, col0:col0 + width]

    def gate(j):
        g0 = n_shift + 2 * d + j * d
        gate_ref[:, j * d:(j + 1) * d] = jax.nn.sigmoid(_dot(hb, w_ref[:, g0:g0 + d]))

    def glu(block):
        lo_col, width = block.start * V7X_LANES, len(block) * V7X_LANES
        u1 = _dot(hb, w_ref[:, n_shift + lo_col:n_shift + lo_col + width])
        u2 = _dot(hb, w_ref[:, n_shift + d + lo_col:n_shift + d + lo_col + width])
        hist_ref[CONV_HALO:CONV_HALO + tm, lo_col:lo_col + width] = u1 * jax.nn.sigmoid(u2)

    def conv(block):
        for s in block:
            _causal_conv_slab(hist_ref, cw_ref, cb_ref, z_ref, slice(s * V7X_LANES, (s + 1) * V7X_LANES), tm)

    n_rkv = rkv_ref.shape[1]
    n_slab = d // V7X_LANES
    assert n_rkv == 3 * d and gate_ref.shape[1] == 2 * d and n_slab % CONV_GROUPS == 0
    per = n_slab // CONV_GROUPS
    blocks = [range(i * per, (i + 1) * per) for i in range(CONV_GROUPS)]
    always = pl.program_id(0) >= 0
    glu(range(n_slab))

    @pl.when(always)
    def _():
        shifted(0, d, rkv_ref, 0)
        conv(blocks[0])

    @pl.when(always)
    def _():
        shifted(d, d, rkv_ref, d)
        conv(blocks[1])

    @pl.when(always)
    def _():
        shifted(2 * d, d, rkv_ref, 2 * d)
        conv(blocks[2])

    @pl.when(always)
    def _():
        shifted(n_rkv, lo_ref.shape[1], lo_ref, 0)
        gate(0)
        conv(blocks[3])
        gate(1)
        hist_ref[0:CONV_HALO, :] = hist_ref[tm:tm + CONV_HALO, :]


def _in_projection(x2, mod_l, gain, mu_all, cw, cb, w_all, layer, *, seq, d, n_rkv, n_gate):
    rows = x2.shape[0]
    tm = min(ROW_TILE, seq)
    tiles_per_seq = seq // tm
    row_spec = lambda width: pl.BlockSpec((tm, width), lambda i: (i, 0))
    return pl.pallas_call(
        functools.partial(_inproj_kernel, tiles_per_seq=tiles_per_seq, d=d),
        grid=(rows // tm,),
        in_specs=[
            row_spec(d),
            pl.BlockSpec((1, 1, mod_l.shape[-1]), lambda i: (i // tiles_per_seq, 0, 0)),
            _const_spec((1, d)),
            _const_spec((1, n_rkv + LO_WIDTH)),
            _const_spec(cw.shape),
            _const_spec((1, d)),
            _layer_spec(w_all.shape, layer),
        ],
        out_specs=[row_spec(n_rkv), row_spec(LO_WIDTH), row_spec(d), row_spec(n_gate)],
        out_shape=[
            jax.ShapeDtypeStruct((rows, n_rkv), F32),
            jax.ShapeDtypeStruct((rows, LO_WIDTH), F32),
            jax.ShapeDtypeStruct((rows, d), F32),
            jax.ShapeDtypeStruct((rows, n_gate), F32),
        ],
        scratch_shapes=[pltpu.VMEM((8, n_rkv + LO_WIDTH), F32), pltpu.VMEM((CONV_HALO + tm, d), F32)],
        compiler_params=pltpu.CompilerParams(dimension_semantics=("arbitrary",), vmem_limit_bytes=VMEM_LIMIT),
        name="in_projection",
    )(x2, mod_l, gain, mu_all, cw, cb, w_all)


def _stack_heads(x, lane_is_head0):
    zero = jnp.zeros_like(x)
    return jnp.concatenate([jnp.where(lane_is_head0, x, zero), jnp.where(lane_is_head0, zero, x)], axis=0)


def _rwkv_kernel(*refs, d, has_vres):
    if has_vres:
        (rkv_ref, lo_ref, vfirst_ref, pvec_ref, wd_ref, wa_ref, wg_ref, wv_ref, tri_ref, ones_ref,
         out_ref, state_ref, y_ref) = refs
    else:
        (rkv_ref, lo_ref, pvec_ref, wd_ref, wa_ref, wg_ref, tri_ref, ones_ref,
         out_ref, state_ref, y_ref) = refs
    c = CHUNK
    n2 = 2 * c
    n_slab = d // V7X_LANES
    slabs = range(n_slab)
    lanes = [slice(s * V7X_LANES, (s + 1) * V7X_LANES) for s in slabs]

    @pl.when(pl.program_id(1) == 0)
    def _():
        state_ref[...] = jnp.zeros_like(state_ref)

    w0, a0, k_k, k_a = (pvec_ref[i:i + 1, :] for i in range(4))
    r_k, gn_gain, gn_bias, v0 = (pvec_ref[i:i + 1, :] for i in range(4, 8))
    lane_is_head0 = lax.broadcasted_iota(jnp.int32, (c, V7X_LANES), 1) < HEAD_SIZE
    ri = lax.broadcasted_iota(jnp.int32, (n2, n2), 0) & (c - 1)
    ci = lax.broadcasted_iota(jnp.int32, (n2, n2), 1) & (c - 1)
    strict = ri > ci
    incl = ri >= ci
    eye = (lax.broadcasted_iota(jnp.int32, (n2, n2), 0) ==
           lax.broadcasted_iota(jnp.int32, (n2, n2), 1)).astype(F32)

    def head_sum(x):
        wide = ones_ref.shape[0]
        xs = jnp.concatenate([x[:, j:j + wide] for j in range(0, d, wide)], axis=0)
        ss = _dot(xs.astype(BF16), ones_ref[...])
        return jnp.concatenate([ss[j * c:(j + 1) * c, :] for j in range(d // wide)], axis=1)

    def prep(q, rows, p):
        r = rkv_ref[q, rows, 0:d]
        k = rkv_ref[q, rows, d:2 * d]
        v = rkv_ref[q, rows, 2 * d:3 * d]
        lo_wa = lo_ref[q, rows, 0:V7X_LANES]
        w_pre = w0 + _dot(jnp.tanh(lo_wa).astype(BF16), wd_ref[...])
        a_sig = jax.nn.sigmoid(a0 + _dot(lo_wa.astype(BF16), wa_ref[...]))
        p["gate"] = _dot(jax.nn.sigmoid(lo_ref[q, rows, V7X_LANES:2 * V7X_LANES]).astype(BF16), wg_ref[...])
        if has_vres:
            mix = jax.nn.sigmoid(v0 + _dot(lo_ref[q, rows, 2 * V7X_LANES:3 * V7X_LANES].astype(BF16), wv_ref[...]))
            v = v + (vfirst_ref[q, rows, :] - v) * mix
        yield
        z = -w_pre
        softplus = jnp.maximum(z, 0.0) + jnp.log1p(jnp.exp(-jnp.abs(z)))
        logw = -jnp.exp(-softplus - 0.5)
        kk = k * k_k
        kk = kk * jnp.minimum(lax.rsqrt(head_sum(kk * kk)), 1e12)
        k = k * (1.0 + (a_sig - 1.0) * k_a)
        b_vec = kk * a_sig
        p["r"], p["k"], p["v"] = r, k, v
        yield
        tri = tri_ref[...]
        lh, lm, ll = _split3(logw)
        cum = _dot(tri, lh) + _dot(tri, lm) + _dot(tri, ll)
        tot = cum[c - 1:c, :]
        g_tot = jnp.exp(tot)
        p["g_tot"] = g_tot
        yield
        e_inv = jnp.exp(-cum)
        stack = lambda x: [_stack_heads(x.astype(BF16)[:, sl], lane_is_head0) for sl in lanes]
        p["a_s"] = stack(-kk * jnp.exp(cum - logw))
        yield
        p["r_s"] = stack(r * jnp.exp(cum))
        yield
        b_t = b_vec * e_inv
        p["b_s"] = stack(b_t)
        yield
        k_t = k * e_inv
        p["k_s"] = stack(k_t)
        yield
        p["v_s"] = stack(v)
        yield
        p["bh_s"] = stack(b_t * g_tot)
        yield
        p["kh_s"] = stack(k_t * g_tot)

    def mix(q, rows, p):
        a_s, r_s, b_s, k_s, v_s = p["a_s"], p["r_s"], p["b_s"], p["k_s"], p["v_s"]
        zz = [_dot_nt(jnp.concatenate([a_s[s], r_s[s]], axis=0), jnp.concatenate([b_s[s], k_s[s]], axis=0))
              for s in slabs]
        yield
        a_ab = [jnp.where(strict, z[0:n2, 0:n2], 0.0) for z in zz]
        a_ak = [jnp.where(strict, z[0:n2, n2:2 * n2], 0.0).astype(BF16) for z in zz]
        m_rb = [jnp.where(incl, z[n2:2 * n2, 0:n2], 0.0).astype(BF16) for z in zz]
        m_rk = [jnp.where(incl, z[n2:2 * n2, n2:2 * n2], 0.0).astype(BF16) for z in zz]
        pb = [a.astype(BF16) for a in a_ab]
        t_inv = [eye + a for a in a_ab]
        pw = [_dot(m, m) for m in pb]
        yield
        akv = [_dot(a_ak[s], v_s[s]).astype(BF16) for s in slabs]
        sb = [state_ref[q, s].astype(BF16) for s in slabs]
        y_rs = [_dot_nt(r_s[s], sb[s]) + _dot(m_rk[s], v_s[s]) for s in slabs]
        yield
        steps = CHUNK.bit_length() - 2
        for it in range(steps):
            pb = [m.astype(BF16) for m in pw]
            if it + 1 < steps:
                both = [_dot(pb[s], jnp.concatenate([t_inv[s].astype(BF16), pb[s]], axis=1)) for s in slabs]
                t_inv = [t_inv[s] + both[s][:, 0:n2] for s in slabs]
                pw = [b2[:, n2:2 * n2] for b2 in both]
            else:
                t_inv = [t_inv[s] + _dot(pb[s], t_inv[s].astype(BF16)) for s in slabs]
            yield
        wu = [_dot(t_inv[s].astype(BF16), jnp.concatenate([a_s[s], akv[s]], axis=1)) for s in slabs]
        yield
        ub = [(_dot_nt(wu[s][:, 0:V7X_LANES].astype(BF16), sb[s]) + wu[s][:, V7X_LANES:2 * V7X_LANES]).astype(BF16)
              for s in slabs]
        yield
        for s in slabs:
            y_s = y_rs[s] + _dot(m_rb[s], ub[s])
            y_ref[q, rows, lanes[s]] = y_s[0:c, :] + y_s[c:2 * c, :]
        yield
        for s in slabs:
            state_ref[q, s] = state_ref[q, s] * p["g_tot"][:, lanes[s]] + _dot_tn(
                jnp.concatenate([ub[s], v_s[s]], axis=0), jnp.concatenate([p["bh_s"][s], p["kh_s"][s]], axis=0))

    def post(q, rows, p):
        y = y_ref[q, rows, :]
        inv_n = 1.0 / HEAD_SIZE
        mu = head_sum(y) * inv_n
        yield
        yc = y - mu
        var = head_sum(yc * yc) * inv_n
        yield
        yn = yc * lax.rsqrt(var + GN_EPS) * gn_gain + gn_bias
        bonus = head_sum(p["r"] * p["k"] * r_k) * p["v"]
        out_ref[q, rows, :] = (yn + bonus) * p["gate"]

    def lockstep(gens):
        live = list(gens)
        while live:
            live = [g for g in live if next(g, StopIteration) is not StopIteration]
            if live:
                yield

    def drive(main, fillers):
        for _ in main:
            for f in fillers:
                next(f, None)
        for f in fillers:
            for _ in f:
                pass

    streams = range(SEQS_PER_STEP)
    row_slices = [slice(i * c, (i + 1) * c) for i in range(CHUNKS_PER_STEP)]
    params = [[dict() for _ in row_slices] for _ in streams]
    drive(lockstep([prep(q, row_slices[0], params[q][0]) for q in streams]), [])
    for i, rows in enumerate(row_slices):
        fillers = []
        if i + 1 < len(row_slices):
            fillers += [prep(q, row_slices[i + 1], params[q][i + 1]) for q in streams]
        if i > 0:
            fillers += [post(q, row_slices[i - 1], params[q][i - 1]) for q in streams]
        drive(lockstep([mix(q, rows, params[q][i]) for q in streams]), fillers)
    drive(lockstep([post(q, row_slices[-1], params[q][-1]) for q in streams]), [])


def _rwkv_time_mix(rkv, lo, v_first_src, pvec, wd, wa, wg, wv, tri, ones_bd, *, bsz, seq, d):
    has_vres = v_first_src is not None
    step_rows = CHUNK * CHUNKS_PER_STEP
    steps = seq // step_rows
    n_slab = d // V7X_LANES
    nq = SEQS_PER_STEP
    blk = lambda width, col: pl.BlockSpec((nq, step_rows, width), lambda b, t: (b, t, col))
    in_specs = [blk(3 * d, 0), blk(LO_WIDTH, 0)]
    args = [rkv.reshape(bsz, seq, 3 * d), lo.reshape(bsz, seq, LO_WIDTH)]
    if has_vres:
        in_specs.append(blk(d, 2))
        args.append(v_first_src.reshape(bsz, seq, 3 * d))
    consts = [pvec, wd, wa, wg] + ([wv] if has_vres else []) + [tri, ones_bd]
    in_specs += [_const_spec(a.shape) for a in consts]
    args += consts
    out = pl.pallas_call(
        functools.partial(_rwkv_kernel, d=d, has_vres=has_vres),
        grid=(bsz // nq, steps),
        in_specs=in_specs,
        out_specs=blk(d, 0),
        out_shape=jax.ShapeDtypeStruct((bsz, seq, d), F32),
        scratch_shapes=[pltpu.VMEM((nq, n_slab, V7X_LANES, V7X_LANES), F32), pltpu.VMEM((nq, step_rows, d), F32)],
        compiler_params=pltpu.CompilerParams(dimension_semantics=("arbitrary", "arbitrary"),
                                             vmem_limit_bytes=VMEM_LIMIT),
        name="rwkv7_time_mix",
    )(*args)
    return out.reshape(bsz * seq, d)


def _mix_mlp_kernel(ya_ref, z_ref, gate_ref, x_ref, mod_ref, cvec_ref, wout_ref, w1_ref, w2_ref, out_ref,
                    *, d, final_norm):
    tm = x_ref.shape[0]
    ln_gain, ln_bias = cvec_ref[0:1, :], cvec_ref[1:2, :]
    ffn_gain, final_gain = cvec_ref[2:3, :], cvec_ref[3:4, :]
    mod = mod_ref[0]
    gt_m = mod[:, 2 * d:3 * d]
    sh_f, sc_f, gt_f = mod[:, 3 * d:4 * d], mod[:, 4 * d:5 * d], mod[:, 5 * d:6 * d]

    m_a = (ya_ref[...] * gate_ref[:, 0:d]).astype(BF16)
    o_a = _dot(m_a, wout_ref[0:d, :])
    z = z_ref[...]
    mu = jnp.mean(z, axis=-1, keepdims=True)
    zc = z - mu
    var = jnp.mean(zc * zc, axis=-1, keepdims=True)
    zn = zc * lax.rsqrt(var + LN_EPS) * ln_gain + ln_bias
    y_b = zn * jax.nn.sigmoid(zn)
    m_b = (y_b * gate_ref[:, d:2 * d]).astype(BF16)
    x1 = x_ref[...] + gt_m * (o_a + _dot(m_b, wout_ref[d:2 * d, :]))

    ms = jnp.mean(x1 * x1, axis=-1, keepdims=True)
    h = ((x1 * lax.rsqrt(ms + RMS_EPS)) * ffn_gain * (1.0 + sc_f) + sh_f).astype(BF16)
    d_ff = w1_ref.shape[1]
    acc = jnp.zeros((tm, d), F32)
    for j in range(d_ff // d):
        hid = jnp.maximum(_dot(h, w1_ref[:, j * d:(j + 1) * d]), 0.0)
        acc = acc + _dot((hid * hid).astype(BF16), w2_ref[j * d:(j + 1) * d, :])
    x2 = x1 + gt_f * acc
    if final_norm:
        ms2 = jnp.mean(x2 * x2, axis=-1, keepdims=True)
        x2 = (x2 * lax.rsqrt(ms2 + RMS_EPS)) * final_gain
    out_ref[...] = x2


def _mix_mlp(ya, z, gates, x2, mod_l, cvec, wout, w1, w2, layer, *, seq, d, final_norm):
    rows = x2.shape[0]
    tm = min(MLP_ROW_TILE, seq)
    tiles_per_seq = seq // tm
    row_spec = lambda width: pl.BlockSpec((tm, width), lambda i: (i, 0))
    return pl.pallas_call(
        functools.partial(_mix_mlp_kernel, d=d, final_norm=final_norm),
        grid=(rows // tm,),
        in_specs=[
            row_spec(d), row_spec(d), row_spec(2 * d), row_spec(d),
            pl.BlockSpec((1, 1, mod_l.shape[-1]), lambda i: (i // tiles_per_seq, 0, 0)),
            _const_spec(cvec.shape), _layer_spec(wout.shape, layer), _layer_spec(w1.shape, layer),
            _layer_spec(w2.shape, layer),
        ],
        out_specs=row_spec(d),
        out_shape=jax.ShapeDtypeStruct((rows, d), F32),
        compiler_params=pltpu.CompilerParams(dimension_semantics=("arbitrary",), vmem_limit_bytes=VMEM_LIMIT),
        name="merge_mlp",
    )(ya, z, gates, x2, mod_l, cvec, wout, w1, w2)


def _pad_rows(w, rows, row0=0):
    out = jnp.zeros((rows, w.shape[1]), w.dtype)
    return lax.dynamic_update_slice(out, w, (row0, 0))


def kernel(x, c, norm_mix_gain, norm_ffn_gain, ada_w, ada_b, w_in, w_in_vres, mu_shift, mu_vres, w0, w_decay_up, a0, w_aaa_up, w_gate_up, k_k, k_a, r_k, gn_gain, gn_bias, v0, w_vres_up, conv_w, conv_b, conv_ln_gain, conv_ln_bias, w_out, w_ff_in, w_ff_out, final_gain):
    bsz, seq, d = x.shape
    depth = w_in.shape[0]
    n_rkv = 3 * d
    n_lo = DECAY_RANK + AAA_RANK + GATE_RANK
    n_shift = n_rkv + n_lo
    n_gate = 2 * d
    assert bsz % SEQS_PER_STEP == 0 and d % V7X_MXU_WIDTH == 0 and seq % (CHUNK * CHUNKS_PER_STEP) == 0 and seq % min(ROW_TILE, seq) == 0 and seq % min(MLP_ROW_TILE, seq) == 0
    assert w_in.shape[2] == n_shift + 2 * d + n_gate and ada_w.shape[2] % MOD_COL_TILE == 0

    mod = _modulation(c, ada_w, ada_b)
    tri = jnp.tril(jnp.ones((CHUNK, CHUNK), BF16))
    lane = jnp.arange(V7X_MXU_WIDTH) // HEAD_SIZE
    ones_bd = (lane[:, None] == lane[None, :]).astype(BF16)

    lo_pad = LO_WIDTH - n_lo
    w_tail = jnp.concatenate([jnp.zeros((1, d, lo_pad), F32),
                              jnp.pad(w_in_vres, ((0, 0), (0, 0), (0, lo_pad - VRES_RANK)))], axis=0)
    mu_tail = jnp.concatenate([jnp.zeros((1, lo_pad), F32), jnp.pad(mu_vres, ((0, 0), (0, lo_pad - VRES_RANK)))], axis=0)
    w_all = jnp.concatenate([w_in[:, :, :n_shift], w_tail, w_in[:, :, n_shift:]], axis=2).astype(BF16)
    mu_all = jnp.concatenate([mu_shift, mu_tail], axis=1)
    w_out_b, w_ff_in_b, w_ff_out_b = w_out.astype(BF16), w_ff_in.astype(BF16), w_ff_out.astype(BF16)

    x2 = x.reshape(bsz * seq, d)
    v_first_src = None
    for l in range(depth):
        has_vres = l > 0
        mod_l = mod[l].reshape(bsz, 1, 6 * d)
        rkv, lo, z, gates = _in_projection(x2, mod_l, norm_mix_gain[l].reshape(1, d), mu_all[l].reshape(1, -1),
                                           _pad_rows(conv_w[l], CONV_HALO), conv_b[l].reshape(1, d), w_all, l,
                                           seq=seq, d=d, n_rkv=n_rkv, n_gate=n_gate)

        pvec = jnp.stack([w0[l], a0[l], k_k[l], k_a[l], r_k[l].reshape(d), gn_gain[l], gn_bias[l],
                          v0[l - 1] if has_vres else jnp.zeros((d,), F32)])
        wd = _pad_rows(w_decay_up[l], V7X_LANES, 0).astype(BF16)
        wa = _pad_rows(w_aaa_up[l], V7X_LANES, DECAY_RANK).astype(BF16)
        wg = w_gate_up[l].astype(BF16)
        wv = _pad_rows(w_vres_up[l - 1], V7X_LANES, 0).astype(BF16) if has_vres else None
        ya = _rwkv_time_mix(rkv, lo, v_first_src, pvec, wd, wa, wg, wv, tri, ones_bd, bsz=bsz, seq=seq, d=d)
        if l == 0:
            v_first_src = rkv

        cvec = jnp.stack([conv_ln_gain[l], conv_ln_bias[l], norm_ffn_gain[l], final_gain] + [jnp.zeros((d,), F32)] * 4)
        x2 = _mix_mlp(ya, z, gates, x2, mod_l, cvec, w_out_b, w_ff_in_b, w_ff_out_b, l,
                      seq=seq, d=d, final_norm=(l == depth - 1))
    return x2.reshape(bsz, seq, d)
```

```python
import functools

import jax
import jax.numpy as jnp
from jax import lax
from jax.experimental import pallas as pl
from jax.experimental.pallas import tpu as pltpu

F32 = jnp.float32
BF16 = jnp.bfloat16

V7X_LANES = 128
V7X_VMEM_BYTES = 64 * 1024 * 1024
V7X_MXU_WIDTH = 256

HEAD_SIZE = 64
HEADS_PER_SLAB = V7X_LANES // HEAD_SIZE
CHUNK = 64
CHUNKS_PER_STEP = 4
SEQS_PER_STEP = 2
DECAY_RANK = 64
AAA_RANK = 64
GATE_RANK = 128
VRES_RANK = 32
LO_WIDTH = 3 * V7X_LANES
CONV_WIDTH = 31
CONV_HALO = 32
CONV_ROWS = 64
CONV_GROUPS = 4

RMS_EPS = 1e-6
LN_EPS = 1e-5
GN_EPS = 64e-5

ROW_TILE = 256
MLP_ROW_TILE = 512
MOD_COL_TILE = 1536
VMEM_LIMIT = 56 * 1024 * 1024


def _dot(a, b):
    return jnp.dot(a, b, preferred_element_type=F32)


def _dot_nt(a, b):
    return lax.dot_general(a, b, (((1,), (1,)), ((), ())), preferred_element_type=F32)


def _dot_tn(a, b):
    return lax.dot_general(a, b, (((0,), (0,)), ((), ())), preferred_element_type=F32)


def _split3(x):
    hi = x.astype(BF16)
    r1 = x - hi.astype(F32)
    mid = r1.astype(BF16)
    lo = (r1 - mid.astype(F32)).astype(BF16)
    return hi, mid, lo


def _const_spec(shape):
    nd = len(shape)
    return pl.BlockSpec(shape, lambda *_: (0,) * nd, pipeline_mode=pl.Buffered(1))


def _layer_spec(stacked_shape, layer):
    nd = len(stacked_shape) - 1
    return pl.BlockSpec((None,) + tuple(stacked_shape[1:]), lambda *_: (layer,) + (0,) * nd,
                        pipeline_mode=pl.Buffered(1))


def _mod_kernel(c_ref, w_ref, b_ref, o_ref):
    c = c_ref[...]
    ca = c * jax.nn.sigmoid(c)
    ch, cm, _ = _split3(ca)
    wh, wm, _ = _split3(w_ref[0])
    o_ref[0] = _dot(ch, wh) + _dot(ch, wm) + _dot(cm, wh) + b_ref[0]


def _modulation(c, ada_w, ada_b):
    depth, d, n = ada_w.shape
    bsz = c.shape[0]
    return pl.pallas_call(
        _mod_kernel,
        grid=(depth, n // MOD_COL_TILE),
        in_specs=[
            pl.BlockSpec((bsz, d), lambda l, j: (0, 0)),
            pl.BlockSpec((1, d, MOD_COL_TILE), lambda l, j: (l, 0, j)),
            pl.BlockSpec((1, 1, MOD_COL_TILE), lambda l, j: (l, 0, j)),
        ],
        out_specs=pl.BlockSpec((1, bsz, MOD_COL_TILE), lambda l, j: (l, 0, j)),
        out_shape=jax.ShapeDtypeStruct((depth, bsz, n), F32),
        compiler_params=pltpu.CompilerParams(dimension_semantics=("arbitrary", "arbitrary")),
        name="adaln_modulation",
    )(c, ada_w, ada_b.reshape(depth, 1, n))


def _causal_conv_slab(hist_ref, cw_ref, cb_ref, out_ref, sl, tm):
    base = CONV_HALO - (CONV_WIDTH - 1)
    n = CONV_ROWS + CONV_HALO
    for g in range(tm // CONV_ROWS):
        r0 = g * CONV_ROWS
        x = hist_ref[r0:r0 + n, sl]
        acc = jnp.broadcast_to(cb_ref[:, sl], (CONV_ROWS, V7X_LANES))
        for res in range(8):
            xr = x if res == 0 else pltpu.roll(x, n - res, 0)
            for j in range(CONV_WIDTH):
                if (base + j) % 8 == res:
                    q0 = base + j - res
                    acc = acc + cw_ref[j:j + 1, sl] * xr[q0:q0 + CONV_ROWS, :]
        out_ref[r0:r0 + CONV_ROWS, sl] = acc


def _inproj_kernel(x_ref, mod_ref, gain_ref, mu_ref, cw_ref, cb_ref, w_ref, wtail_ref, rkv_ref, lo_ref, z_ref,
                   gate_ref, carry_ref, hist_ref, *, tiles_per_seq, d):
    tm = x_ref.shape[0]
    n_rkv = rkv_ref.shape[1]
    n_lo = lo_ref.shape[1] - wtail_ref.shape[1]
    n_shift = n_rkv + n_lo

    @pl.when(pl.program_id(0) % tiles_per_seq == 0)
    def _():
        carry_ref[...] = jnp.zeros_like(carry_ref)
        hist_ref[0:CONV_HALO, :] = jnp.zeros((CONV_HALO, d), F32)

    x = x_ref[...]
    mod = mod_ref[0]
    shift, scale = mod[:, 0:d], mod[:, d:2 * d]
    ms = jnp.mean(x * x, axis=-1, keepdims=True)
    h = (x * lax.rsqrt(ms + RMS_EPS)) * gain_ref[...] * (1.0 + scale) + shift
    hb = h.astype(BF16)
    first_row = lax.broadcasted_iota(jnp.int32, (tm, 1), 0) == 0

    def shifted(col0, width, out_ref, out_col0, w_src=None):
        p = _dot(hb, w_ref[:, col0:col0 + width] if w_src is None else w_src[...])
        prev = jnp.where(first_row, carry_ref[0:1, col0:col0 + width], pltpu.roll(p, 1, 0))
        carry_ref[0:1, col0:col0 + width] = p[tm - 1:tm, :]
        out_ref[:, out_col0:out_col0 + width] = p + (prev - p) * mu_ref[:, col0:col0 + width]

    def gate(j):
        g0 = n_shift + 2 * d + j * d
        gate_ref[:, j * d:(j + 1) * d] = jax.nn.sigmoid(_dot(hb, w_ref[:, g0:g0 + d]))

    def glu(block):
        lo_col, width = block.start * V7X_LANES, len(block) * V7X_LANES
        u1 = _dot(hb, w_ref[:, n_shift + lo_col:n_shift + lo_col + width])
        u2 = _dot(hb, w_ref[:, n_shift + d + lo_col:n_shift + d + lo_col + width])
        hist_ref[CONV_HALO:CONV_HALO + tm, lo_col:lo_col + width] = u1 * jax.nn.sigmoid(u2)

    def conv(block):
        for s in block:
            _causal_conv_slab(hist_ref, cw_ref, cb_ref, z_ref, slice(s * V7X_LANES, (s + 1) * V7X_LANES), tm)

    n_slab = d // V7X_LANES
    assert n_rkv == 3 * d and gate_ref.shape[1] == 2 * d and n_slab % CONV_GROUPS == 0
    per = n_slab // CONV_GROUPS
    blocks = [range(i * per, (i + 1) * per) for i in range(CONV_GROUPS)]
    always = pl.program_id(0) >= 0
    glu(range(n_slab))

    @pl.when(always)
    def _():
        shifted(0, d, rkv_ref, 0)
        conv(blocks[0])

    @pl.when(always)
    def _():
        shifted(d, d, rkv_ref, d)
        conv(blocks[1])

    @pl.when(always)
    def _():
        shifted(2 * d, d, rkv_ref, 2 * d)
        conv(blocks[2])

    @pl.when(always)
    def _():
        shifted(n_rkv, n_lo, lo_ref, 0)
        shifted(n_shift, wtail_ref.shape[1], lo_ref, n_lo, wtail_ref)
        gate(0)
        conv(blocks[3])
        gate(1)
        hist_ref[0:CONV_HALO, :] = hist_ref[tm:tm + CONV_HALO, :]


def _in_projection(x2, mod_l, gain, mu_all, cw, cb, w_in_b, w_tail_b, layer, *, seq, d, n_rkv, n_gate):
    rows = x2.shape[0]
    tm = min(ROW_TILE, seq)
    tiles_per_seq = seq // tm
    row_spec = lambda width: pl.BlockSpec((tm, width), lambda i: (i, 0))
    return pl.pallas_call(
        functools.partial(_inproj_kernel, tiles_per_seq=tiles_per_seq, d=d),
        grid=(rows // tm,),
        in_specs=[
            row_spec(d),
            pl.BlockSpec((1, 1, mod_l.shape[-1]), lambda i: (i // tiles_per_seq, 0, 0)),
            _const_spec((1, d)),
            _const_spec((1, n_rkv + LO_WIDTH)),
            _const_spec(cw.shape),
            _const_spec((1, d)),
            _layer_spec(w_in_b.shape, layer),
            _layer_spec(w_tail_b.shape, layer),
        ],
        out_specs=[row_spec(n_rkv), row_spec(LO_WIDTH), row_spec(d), row_spec(n_gate)],
        out_shape=[
            jax.ShapeDtypeStruct((rows, n_rkv), F32),
            jax.ShapeDtypeStruct((rows, LO_WIDTH), F32),
            jax.ShapeDtypeStruct((rows, d), F32),
            jax.ShapeDtypeStruct((rows, n_gate), F32),
        ],
        scratch_shapes=[pltpu.VMEM((8, n_rkv + LO_WIDTH), F32), pltpu.VMEM((CONV_HALO + tm, d), F32)],
        compiler_params=pltpu.CompilerParams(dimension_semantics=("arbitrary",), vmem_limit_bytes=VMEM_LIMIT),
        name="in_projection",
    )(x2, mod_l, gain, mu_all, cw, cb, w_in_b, w_tail_b)


def _stack_heads(x, lane_is_head0):
    zero = jnp.zeros_like(x)
    return jnp.concatenate([jnp.where(lane_is_head0, x, zero), jnp.where(lane_is_head0, zero, x)], axis=0)


def _rwkv_kernel(*refs, d, has_vres):
    if has_vres:
        (rkv_ref, lo_ref, vfirst_ref, pvec_ref, wd_ref, wa_ref, wg_ref, wv_ref, tri_ref, ones_ref,
         out_ref, state_ref, y_ref) = refs
    else:
        (rkv_ref, lo_ref, pvec_ref, wd_ref, wa_ref, wg_ref, tri_ref, ones_ref,
         out_ref, state_ref, y_ref) = refs
    c = CHUNK
    n2 = 2 * c
    n_slab = d // V7X_LANES
    slabs = range(n_slab)
    lanes = [slice(s * V7X_LANES, (s + 1) * V7X_LANES) for s in slabs]

    @pl.when(pl.program_id(1) == 0)
    def _():
        state_ref[...] = jnp.zeros_like(state_ref)

    w0, a0, k_k, k_a = (pvec_ref[i:i + 1, :] for i in range(4))
    r_k, gn_gain, gn_bias, v0 = (pvec_ref[i:i + 1, :] for i in range(4, 8))
    lane_is_head0 = lax.broadcasted_iota(jnp.int32, (c, V7X_LANES), 1) < HEAD_SIZE
    ri = lax.broadcasted_iota(jnp.int32, (n2, n2), 0) & (c - 1)
    ci = lax.broadcasted_iota(jnp.int32, (n2, n2), 1) & (c - 1)
    strict = ri > ci
    incl = ri >= ci
    eye = (lax.broadcasted_iota(jnp.int32, (n2, n2), 0) ==
           lax.broadcasted_iota(jnp.int32, (n2, n2), 1)).astype(F32)

    def head_sum(x):
        wide = ones_ref.shape[0]
        xs = jnp.concatenate([x[:, j:j + wide] for j in range(0, d, wide)], axis=0)
        ss = _dot(xs.astype(BF16), ones_ref[...])
        return jnp.concatenate([ss[j * c:(j + 1) * c, :] for j in range(d // wide)], axis=1)

    def prep(q, rows, p):
        r = rkv_ref[q, rows, 0:d]
        k = rkv_ref[q, rows, d:2 * d]
        v = rkv_ref[q, rows, 2 * d:3 * d]
        lo_wa = lo_ref[q, rows, 0:V7X_LANES]
        w_pre = w0 + _dot(jnp.tanh(lo_wa).astype(BF16), wd_ref[...])
        a_sig = jax.nn.sigmoid(a0 + _dot(lo_wa.astype(BF16), wa_ref[...]))
        p["gate"] = _dot(jax.nn.sigmoid(lo_ref[q, rows, V7X_LANES:2 * V7X_LANES]).astype(BF16), wg_ref[...])
        if has_vres:
            mix = jax.nn.sigmoid(v0 + _dot(lo_ref[q, rows, 2 * V7X_LANES:3 * V7X_LANES].astype(BF16), wv_ref[...]))
            v = v + (vfirst_ref[q, rows, :] - v) * mix
        yield
        z = -w_pre
        softplus = jnp.maximum(z, 0.0) + jnp.log1p(jnp.exp(-jnp.abs(z)))
        logw = -jnp.exp(-softplus - 0.5)
        kk = k * k_k
        kk = kk * jnp.minimum(lax.rsqrt(head_sum(kk * kk)), 1e12)
        k = k * (1.0 + (a_sig - 1.0) * k_a)
        b_vec = kk * a_sig
        p["r"], p["k"], p["v"] = r, k, v
        yield
        tri = tri_ref[...]
        lh, lm, ll = _split3(logw)
        cum = _dot(tri, lh) + _dot(tri, lm) + _dot(tri, ll)
        tot = cum[c - 1:c, :]
        g_tot = jnp.exp(tot)
        p["g_tot"] = g_tot
        yield
        e_inv = jnp.exp(-cum)
        stack = lambda x: [_stack_heads(x.astype(BF16)[:, sl], lane_is_head0) for sl in lanes]
        p["a_s"] = stack(-kk * jnp.exp(cum - logw))
        yield
        p["r_s"] = stack(r * jnp.exp(cum))
        yield
        b_t = b_vec * e_inv
        p["b_s"] = stack(b_t)
        yield
        k_t = k * e_inv
        p["k_s"] = stack(k_t)
        yield
        p["v_s"] = stack(v)
        yield
        p["bh_s"] = stack(b_t * g_tot)
        yield
        p["kh_s"] = stack(k_t * g_tot)

    def mix(q, rows, p):
        a_s, r_s, b_s, k_s, v_s = p["a_s"], p["r_s"], p["b_s"], p["k_s"], p["v_s"]
        zz = [_dot_nt(jnp.concatenate([a_s[s], r_s[s]], axis=0), jnp.concatenate([b_s[s], k_s[s]], axis=0))
              for s in slabs]
        yield
        a_ab = [jnp.where(strict, z[0:n2, 0:n2], 0.0) for z in zz]
        a_ak = [jnp.where(strict, z[0:n2, n2:2 * n2], 0.0).astype(BF16) for z in zz]
        m_rb = [jnp.where(incl, z[n2:2 * n2, 0:n2], 0.0).astype(BF16) for z in zz]
        m_rk = [jnp.where(incl, z[n2:2 * n2, n2:2 * n2], 0.0).astype(BF16) for z in zz]
        pb = [a.astype(BF16) for a in a_ab]
        t_inv = [eye + a for a in a_ab]
        pw = [_dot(m, m) for m in pb]
        yield
        akv = [_dot(a_ak[s], v_s[s]).astype(BF16) for s in slabs]
        sb = [state_ref[q, s].astype(BF16) for s in slabs]
        y_rs = [_dot_nt(r_s[s], sb[s]) + _dot(m_rk[s], v_s[s]) for s in slabs]
        yield
        steps = CHUNK.bit_length() - 2
        for it in range(steps):
            pb = [m.astype(BF16) for m in pw]
            if it + 1 < steps:
                both = [_dot(pb[s], jnp.concatenate([t_inv[s].astype(BF16), pb[s]], axis=1)) for s in slabs]
                t_inv = [t_inv[s] + both[s][:, 0:n2] for s in slabs]
                pw = [b2[:, n2:2 * n2] for b2 in both]
            else:
                t_inv = [t_inv[s] + _dot(pb[s], t_inv[s].astype(BF16)) for s in slabs]
            yield
        wu = [_dot(t_inv[s].astype(BF16), jnp.concatenate([a_s[s], akv[s]], axis=1)) for s in slabs]
        yield
        ub = [(_dot_nt(wu[s][:, 0:V7X_LANES].astype(BF16), sb[s]) + wu[s][:, V7X_LANES:2 * V7X_LANES]).astype(BF16)
              for s in slabs]
        yield
        for s in slabs:
            y_s = y_rs[s] + _dot(m_rb[s], ub[s])
            y_ref[q, rows, lanes[s]] = y_s[0:c, :] + y_s[c:2 * c, :]
        yield
        for s in slabs:
            state_ref[q, s] = state_ref[q, s] * p["g_tot"][:, lanes[s]] + _dot_tn(
                jnp.concatenate([ub[s], v_s[s]], axis=0), jnp.concatenate([p["bh_s"][s], p["kh_s"][s]], axis=0))

    def post(q, rows, p):
        y = y_ref[q, rows, :]
        inv_n = 1.0 / HEAD_SIZE
        mu = head_sum(y) * inv_n
        yield
        yc = y - mu
        var = head_sum(yc * yc) * inv_n
        yield
        yn = yc * lax.rsqrt(var + GN_EPS) * gn_gain + gn_bias
        bonus = head_sum(p["r"] * p["k"] * r_k) * p["v"]
        out_ref[q, rows, :] = (yn + bonus) * p["gate"]

    def lockstep(gens):
        live = list(gens)
        while live:
            live = [g for g in live if next(g, StopIteration) is not StopIteration]
            if live:
                yield

    def drive(main, fillers):
        for _ in main:
            for f in fillers:
                next(f, None)
        for f in fillers:
            for _ in f:
                pass

    streams = range(SEQS_PER_STEP)
    row_slices = [slice(i * c, (i + 1) * c) for i in range(CHUNKS_PER_STEP)]
    params = [[dict() for _ in row_slices] for _ in streams]
    drive(lockstep([prep(q, row_slices[0], params[q][0]) for q in streams]), [])
    for i, rows in enumerate(row_slices):
        fillers = []
        if i + 1 < len(row_slices):
            fillers += [prep(q, row_slices[i + 1], params[q][i + 1]) for q in streams]
        if i > 0:
            fillers += [post(q, row_slices[i - 1], params[q][i - 1]) for q in streams]
        drive(lockstep([mix(q, rows, params[q][i]) for q in streams]), fillers)
    drive(lockstep([post(q, row_slices[-1], params[q][-1]) for q in streams]), [])


def _rwkv_time_mix(rkv, lo, v_first_src, pvec, wd, wa, wg, wv, tri, ones_bd, *, bsz, seq, d):
    has_vres = v_first_src is not None
    step_rows = CHUNK * CHUNKS_PER_STEP
    steps = seq // step_rows
    n_slab = d // V7X_LANES
    nq = SEQS_PER_STEP
    blk = lambda width, col: pl.BlockSpec((nq, step_rows, width), lambda b, t: (b, t, col))
    in_specs = [blk(3 * d, 0), blk(LO_WIDTH, 0)]
    args = [rkv.reshape(bsz, seq, 3 * d), lo.reshape(bsz, seq, LO_WIDTH)]
    if has_vres:
        in_specs.append(blk(d, 2))
        args.append(v_first_src.reshape(bsz, seq, 3 * d))
    consts = [pvec, wd, wa, wg] + ([wv] if has_vres else []) + [tri, ones_bd]
    in_specs += [_const_spec(a.shape) for a in consts]
    args += consts
    out = pl.pallas_call(
        functools.partial(_rwkv_kernel, d=d, has_vres=has_vres),
        grid=(bsz // nq, steps),
        in_specs=in_specs,
        out_specs=blk(d, 0),
        out_shape=jax.ShapeDtypeStruct((bsz, seq, d), F32),
        scratch_shapes=[pltpu.VMEM((nq, n_slab, V7X_LANES, V7X_LANES), F32), pltpu.VMEM((nq, step_rows, d), F32)],
        compiler_params=pltpu.CompilerParams(dimension_semantics=("arbitrary", "arbitrary"),
                                             vmem_limit_bytes=VMEM_LIMIT),
        name="rwkv7_time_mix",
    )(*args)
    return out.reshape(bsz * seq, d)


def _mix_mlp_kernel(ya_ref, z_ref, gate_ref, x_ref, mod_ref, cvec_ref, wout_ref, w1_ref, w2_ref, out_ref,
                    *, d, final_norm):
    tm = x_ref.shape[0]
    ln_gain, ln_bias = cvec_ref[0:1, :], cvec_ref[1:2, :]
    ffn_gain, final_gain = cvec_ref[2:3, :], cvec_ref[3:4, :]
    mod = mod_ref[0]
    gt_m = mod[:, 2 * d:3 * d]
    sh_f, sc_f, gt_f = mod[:, 3 * d:4 * d], mod[:, 4 * d:5 * d], mod[:, 5 * d:6 * d]

    m_a = (ya_ref[...] * gate_ref[:, 0:d]).astype(BF16)
    o_a = _dot(m_a, wout_ref[0:d, :])
    z = z_ref[...]
    mu = jnp.mean(z, axis=-1, keepdims=True)
    zc = z - mu
    var = jnp.mean(zc * zc, axis=-1, keepdims=True)
    zn = zc * lax.rsqrt(var + LN_EPS) * ln_gain + ln_bias
    y_b = zn * jax.nn.sigmoid(zn)
    m_b = (y_b * gate_ref[:, d:2 * d]).astype(BF16)
    x1 = x_ref[...] + gt_m * (o_a + _dot(m_b, wout_ref[d:2 * d, :]))

    ms = jnp.mean(x1 * x1, axis=-1, keepdims=True)
    h = ((x1 * lax.rsqrt(ms + RMS_EPS)) * ffn_gain * (1.0 + sc_f) + sh_f).astype(BF16)
    d_ff = w1_ref.shape[1]
    acc = jnp.zeros((tm, d), F32)
    for j in range(d_ff // d):
        hid = jnp.maximum(_dot(h, w1_ref[:, j * d:(j + 1) * d]), 0.0)
        acc = acc + _dot((hid * hid).astype(BF16), w2_ref[j * d:(j + 1) * d, :])
    x2 = x1 + gt_f * acc
    if final_norm:
        ms2 = jnp.mean(x2 * x2, axis=-1, keepdims=True)
        x2 = (x2 * lax.rsqrt(ms2 + RMS_EPS)) * final_gain
    out_ref[...] = x2


def _mix_mlp(ya, z, gates, x2, mod_l, cvec, wout, w1, w2, layer, *, seq, d, final_norm):
    rows = x2.shape[0]
    tm = min(MLP_ROW_TILE, seq)
    tiles_per_seq = seq // tm
    row_spec = lambda width: pl.BlockSpec((tm, width), lambda i: (i, 0))
    return pl.pallas_call(
        functools.partial(_mix_mlp_kernel, d=d, final_norm=final_norm),
        grid=(rows // tm,),
        in_specs=[
            row_spec(d), row_spec(d), row_spec(2 * d), row_spec(d),
            pl.BlockSpec((1, 1, mod_l.shape[-1]), lambda i: (i // tiles_per_seq, 0, 0)),
            _const_spec(cvec.shape), _layer_spec(wout.shape, layer), _layer_spec(w1.shape, layer),
            _layer_spec(w2.shape, layer),
        ],
        out_specs=row_spec(d),
        out_shape=jax.ShapeDtypeStruct((rows, d), F32),
        compiler_params=pltpu.CompilerParams(dimension_semantics=("arbitrary",), vmem_limit_bytes=VMEM_LIMIT),
        name="merge_mlp",
    )(ya, z, gates, x2, mod_l, cvec, wout, w1, w2)


def _pad_rows(w, rows, row0=0):
    out = jnp.zeros((rows, w.shape[1]), w.dtype)
    return lax.dynamic_update_slice(out, w, (row0, 0))


def kernel(x, c, norm_mix_gain, norm_ffn_gain, ada_w, ada_b, w_in, w_in_vres, mu_shift, mu_vres, w0, w_decay_up, a0, w_aaa_up, w_gate_up, k_k, k_a, r_k, gn_gain, gn_bias, v0, w_vres_up, conv_w, conv_b, conv_ln_gain, conv_ln_bias, w_out, w_ff_in, w_ff_out, final_gain):
    bsz, seq, d = x.shape
    depth = w_in.shape[0]
    n_rkv = 3 * d
    n_lo = DECAY_RANK + AAA_RANK + GATE_RANK
    n_shift = n_rkv + n_lo
    n_gate = 2 * d
    assert bsz % SEQS_PER_STEP == 0 and d % V7X_MXU_WIDTH == 0 and seq % (CHUNK * CHUNKS_PER_STEP) == 0 and seq % min(ROW_TILE, seq) == 0 and seq % min(MLP_ROW_TILE, seq) == 0
    assert w_in.shape[2] == n_shift + 2 * d + n_gate and ada_w.shape[2] % MOD_COL_TILE == 0

    mod = _modulation(c, ada_w, ada_b)
    tri = jnp.tril(jnp.ones((CHUNK, CHUNK), BF16))
    lane = jnp.arange(V7X_MXU_WIDTH) // HEAD_SIZE
    ones_bd = (lane[:, None] == lane[None, :]).astype(BF16)

    lo_pad = LO_WIDTH - n_lo
    w_tail_b = jnp.concatenate([jnp.zeros((1, d, lo_pad), F32),
                                jnp.pad(w_in_vres, ((0, 0), (0, 0), (0, lo_pad - VRES_RANK)))], axis=0).astype(BF16)
    mu_tail = jnp.concatenate([jnp.zeros((1, lo_pad), F32), jnp.pad(mu_vres, ((0, 0), (0, lo_pad - VRES_RANK)))], axis=0)
    w_in_b = w_in.astype(BF16)
    mu_all = jnp.concatenate([mu_shift, mu_tail], axis=1)
    w_out_b, w_ff_in_b, w_ff_out_b = w_out.astype(BF16), w_ff_in.astype(BF16), w_ff_out.astype(BF16)

    x2 = x.reshape(bsz * seq, d)
    v_first_src = None
    for l in range(depth):
        has_vres = l > 0
        mod_l = mod[l].reshape(bsz, 1, 6 * d)
        rkv, lo, z, gates = _in_projection(x2, mod_l, norm_mix_gain[l].reshape(1, d), mu_all[l].reshape(1, -1),
                                           _pad_rows(conv_w[l], CONV_HALO), conv_b[l].reshape(1, d), w_in_b, w_tail_b, l,
                                           seq=seq, d=d, n_rkv=n_rkv, n_gate=n_gate)

        pvec = jnp.stack([w0[l], a0[l], k_k[l], k_a[l], r_k[l].reshape(d), gn_gain[l], gn_bias[l],
                          v0[l - 1] if has_vres else jnp.zeros((d,), F32)])
        wd = _pad_rows(w_decay_up[l], V7X_LANES, 0).astype(BF16)
        wa = _pad_rows(w_aaa_up[l], V7X_LANES, DECAY_RANK).astype(BF16)
        wg = w_gate_up[l].astype(BF16)
        wv = _pad_rows(w_vres_up[l - 1], V7X_LANES, 0).astype(BF16) if has_vres else None
        ya = _rwkv_time_mix(rkv, lo, v_first_src, pvec, wd, wa, wg, wv, tri, ones_bd, bsz=bsz, seq=seq, d=d)
        if l == 0:
            v_first_src = rkv

        cvec = jnp.stack([conv_ln_gain[l], conv_ln_bias[l], norm_ffn_gain[l], final_gain] + [jnp.zeros((d,), F32)] * 4)
        x2 = _mix_mlp(ya, z, gates, x2, mod_l, cvec, w_out_b, w_ff_in_b, w_ff_out_b, l,
                      seq=seq, d=d, final_norm=(l == depth - 1))
    return x2.reshape(bsz, seq, d)
```

```python
import functools

import jax
import jax.numpy as jnp
from jax import lax
from jax.experimental import pallas as pl
from jax.experimental.pallas import tpu as pltpu

F32 = jnp.float32
BF16 = jnp.bfloat16

V7X_LANES = 128
V7X_VMEM_BYTES = 64 * 1024 * 1024
V7X_MXU_WIDTH = 256

HEAD_SIZE = 64
HEADS_PER_SLAB = V7X_LANES // HEAD_SIZE
CHUNK = 64
CHUNKS_PER_STEP = 4
SEQS_PER_STEP = 2
DECAY_RANK = 64
AAA_RANK = 64
GATE_RANK = 128
VRES_RANK = 32
LO_WIDTH = 3 * V7X_LANES
CONV_WIDTH = 31
CONV_HALO = 32
CONV_ROWS = 64
CONV_GROUPS = 4

RMS_EPS = 1e-6
LN_EPS = 1e-5
GN_EPS = 64e-5

ROW_TILE = 256
MLP_ROW_TILE = 512
MOD_COL_TILE = 1536
VMEM_LIMIT = 56 * 1024 * 1024


def _dot(a, b):
    return jnp.dot(a, b, preferred_element_type=F32)


def _dot_nt(a, b):
    return lax.dot_general(a, b, (((1,), (1,)), ((), ())), preferred_element_type=F32)


def _dot_tn(a, b):
    return lax.dot_general(a, b, (((0,), (0,)), ((), ())), preferred_element_type=F32)


def _split3(x):
    hi = x.astype(BF16)
    r1 = x - hi.astype(F32)
    mid = r1.astype(BF16)
    lo = (r1 - mid.astype(F32)).astype(BF16)
    return hi, mid, lo


def _const_spec(shape):
    nd = len(shape)
    return pl.BlockSpec(shape, lambda *_: (0,) * nd, pipeline_mode=pl.Buffered(1))


def _layer_spec(stacked_shape, layer):
    nd = len(stacked_shape) - 1
    return pl.BlockSpec((None,) + tuple(stacked_shape[1:]), lambda *_: (layer,) + (0,) * nd,
                        pipeline_mode=pl.Buffered(1))


def _mod_kernel(c_ref, w_ref, b_ref, o_ref):
    c = c_ref[...]
    ca = c * jax.nn.sigmoid(c)
    ch, cm, _ = _split3(ca)
    wh, wm, _ = _split3(w_ref[0])
    o_ref[0] = _dot(ch, wh) + _dot(ch, wm) + _dot(cm, wh) + b_ref[0]


def _modulation(c, ada_w, ada_b):
    depth, d, n = ada_w.shape
    bsz = c.shape[0]
    return pl.pallas_call(
        _mod_kernel,
        grid=(depth, n // MOD_COL_TILE),
        in_specs=[
            pl.BlockSpec((bsz, d), lambda l, j: (0, 0)),
            pl.BlockSpec((1, d, MOD_COL_TILE), lambda l, j: (l, 0, j)),
            pl.BlockSpec((1, 1, MOD_COL_TILE), lambda l, j: (l, 0, j)),
        ],
        out_specs=pl.BlockSpec((1, bsz, MOD_COL_TILE), lambda l, j: (l, 0, j)),
        out_shape=jax.ShapeDtypeStruct((depth, bsz, n), F32),
        compiler_params=pltpu.CompilerParams(dimension_semantics=("arbitrary", "arbitrary")),
        name="adaln_modulation",
    )(c, ada_w, ada_b.reshape(depth, 1, n))


def _causal_conv_slab(hist_ref, cw_ref, cb_ref, out_ref, sl, tm):
    base = CONV_HALO - (CONV_WIDTH - 1)
    n = CONV_ROWS + CONV_HALO
    for g in range(tm // CONV_ROWS):
        r0 = g * CONV_ROWS
        x = hist_ref[r0:r0 + n, sl]
        acc = jnp.broadcast_to(cb_ref[:, sl], (CONV_ROWS, V7X_LANES))
        for res in range(8):
            xr = x if res == 0 else pltpu.roll(x, n - res, 0)
            for j in range(CONV_WIDTH):
                if (base + j) % 8 == res:
                    q0 = base + j - res
                    acc = acc + cw_ref[j:j + 1, sl] * xr[q0:q0 + CONV_ROWS, :]
        out_ref[r0:r0 + CONV_ROWS, sl] = acc


def _inproj_kernel(x_ref, mod_ref, gain_ref, mu_ref, cw_ref, cb_ref, w_ref, wtail_ref, rkv_ref, lo_ref, z_ref,
                   gate_ref, carry_ref, hist_ref, *, tiles_per_seq, d):
    tm = x_ref.shape[0]
    n_rkv = rkv_ref.shape[1]
    n_lo = lo_ref.shape[1] - wtail_ref.shape[1]
    n_shift = n_rkv + n_lo

    @pl.when(pl.program_id(0) % tiles_per_seq == 0)
    def _():
        carry_ref[...] = jnp.zeros_like(carry_ref)
        hist_ref[0:CONV_HALO, :] = jnp.zeros((CONV_HALO, d), F32)

    x = x_ref[...]
    mod = mod_ref[0]
    shift, scale = mod[:, 0:d], mod[:, d:2 * d]
    ms = jnp.mean(x * x, axis=-1, keepdims=True)
    h = (x * lax.rsqrt(ms + RMS_EPS)) * gain_ref[...] * (1.0 + scale) + shift
    hb = h.astype(BF16)
    first_row = lax.broadcasted_iota(jnp.int32, (tm, 1), 0) == 0

    def shifted(col0, width, out_ref, out_col0, w_src=None):
        for c0 in range(0, width, min(width, V7X_MXU_WIDTH)):
            cw = min(V7X_MXU_WIDTH, width - c0)
            w = w_ref[:, col0 + c0:col0 + c0 + cw] if w_src is None else w_src[:, c0:c0 + cw]
            p = _dot(hb, w)
            cols = slice(col0 + c0, col0 + c0 + cw)
            prev = jnp.where(first_row, carry_ref[0:1, cols], pltpu.roll(p, 1, 0))
            carry_ref[0:1, cols] = p[tm - 1:tm, :]
            out_ref[:, out_col0 + c0:out_col0 + c0 + cw] = p + (prev - p) * mu_ref[:, cols]

    def gate(j):
        for c0 in range(0, d, V7X_MXU_WIDTH):
            g0 = n_shift + 2 * d + j * d + c0
            gate_ref[:, j * d + c0:j * d + c0 + V7X_MXU_WIDTH] = jax.nn.sigmoid(
                _dot(hb, w_ref[:, g0:g0 + V7X_MXU_WIDTH]))

    def glu(block):
        lo_col, width = block.start * V7X_LANES, len(block) * V7X_LANES
        u1 = _dot(hb, w_ref[:, n_shift + lo_col:n_shift + lo_col + width])
        u2 = _dot(hb, w_ref[:, n_shift + d + lo_col:n_shift + d + lo_col + width])
        hist_ref[CONV_HALO:CONV_HALO + tm, lo_col:lo_col + width] = u1 * jax.nn.sigmoid(u2)

    def conv(block):
        for s in block:
            _causal_conv_slab(hist_ref, cw_ref, cb_ref, z_ref, slice(s * V7X_LANES, (s + 1) * V7X_LANES), tm)

    n_slab = d // V7X_LANES
    assert n_rkv == 3 * d and gate_ref.shape[1] == 2 * d and n_slab % CONV_GROUPS == 0
    per = n_slab // CONV_GROUPS
    blocks = [range(i * per, (i + 1) * per) for i in range(CONV_GROUPS)]
    always = pl.program_id(0) >= 0
    glu(range(n_slab))

    @pl.when(always)
    def _():
        shifted(0, d, rkv_ref, 0)
        conv(blocks[0])

    @pl.when(always)
    def _():
        shifted(d, d, rkv_ref, d)
        conv(blocks[1])

    @pl.when(always)
    def _():
        shifted(2 * d, d, rkv_ref, 2 * d)
        conv(blocks[2])

    @pl.when(always)
    def _():
        shifted(n_rkv, n_lo, lo_ref, 0)
        shifted(n_shift, wtail_ref.shape[1], lo_ref, n_lo, wtail_ref)
        gate(0)
        conv(blocks[3])
        gate(1)
        hist_ref[0:CONV_HALO, :] = hist_ref[tm:tm + CONV_HALO, :]


def _in_projection(x2, mod_l, gain, mu_all, cw, cb, w_in_b, w_tail_b, layer, *, seq, d, n_rkv, n_gate):
    rows = x2.shape[0]
    tm = min(ROW_TILE, seq)
    tiles_per_seq = seq // tm
    row_spec = lambda width: pl.BlockSpec((tm, width), lambda i: (i, 0))
    return pl.pallas_call(
        functools.partial(_inproj_kernel, tiles_per_seq=tiles_per_seq, d=d),
        grid=(rows // tm,),
        in_specs=[
            row_spec(d),
            pl.BlockSpec((1, 1, mod_l.shape[-1]), lambda i: (i // tiles_per_seq, 0, 0)),
            _const_spec((1, d)),
            _const_spec((1, n_rkv + LO_WIDTH)),
            _const_spec(cw.shape),
            _const_spec((1, d)),
            _layer_spec(w_in_b.shape, layer),
            _layer_spec(w_tail_b.shape, layer),
        ],
        out_specs=[row_spec(n_rkv), row_spec(LO_WIDTH), row_spec(d), row_spec(n_gate)],
        out_shape=[
            jax.ShapeDtypeStruct((rows, n_rkv), F32),
            jax.ShapeDtypeStruct((rows, LO_WIDTH), F32),
            jax.ShapeDtypeStruct((rows, d), F32),
            jax.ShapeDtypeStruct((rows, n_gate), F32),
        ],
        scratch_shapes=[pltpu.VMEM((8, n_rkv + LO_WIDTH), F32), pltpu.VMEM((CONV_HALO + tm, d), F32)],
        compiler_params=pltpu.CompilerParams(dimension_semantics=("arbitrary",), vmem_limit_bytes=VMEM_LIMIT),
        name="in_projection",
    )(x2, mod_l, gain, mu_all, cw, cb, w_in_b, w_tail_b)


def _stack_heads(x, lane_is_head0):
    zero = jnp.zeros_like(x)
    return jnp.concatenate([jnp.where(lane_is_head0, x, zero), jnp.where(lane_is_head0, zero, x)], axis=0)


def _rwkv_kernel(*refs, d, has_vres):
    if has_vres:
        (rkv_ref, lo_ref, vfirst_ref, pvec_ref, wd_ref, wa_ref, wg_ref, wv_ref, tri_ref, ones_ref,
         out_ref, state_ref, y_ref) = refs
    else:
        (rkv_ref, lo_ref, pvec_ref, wd_ref, wa_ref, wg_ref, tri_ref, ones_ref,
         out_ref, state_ref, y_ref) = refs
    c = CHUNK
    n2 = 2 * c
    n_slab = d // V7X_LANES
    slabs = range(n_slab)
    lanes = [slice(s * V7X_LANES, (s + 1) * V7X_LANES) for s in slabs]

    @pl.when(pl.program_id(1) == 0)
    def _():
        state_ref[...] = jnp.zeros_like(state_ref)

    w0, a0, k_k, k_a = (pvec_ref[i:i + 1, :] for i in range(4))
    r_k, gn_gain, gn_bias, v0 = (pvec_ref[i:i + 1, :] for i in range(4, 8))
    lane_is_head0 = lax.broadcasted_iota(jnp.int32, (c, V7X_LANES), 1) < HEAD_SIZE
    ri = lax.broadcasted_iota(jnp.int32, (n2, n2), 0) & (c - 1)
    ci = lax.broadcasted_iota(jnp.int32, (n2, n2), 1) & (c - 1)
    strict = ri > ci
    incl = ri >= ci
    eye = (lax.broadcasted_iota(jnp.int32, (n2, n2), 0) ==
           lax.broadcasted_iota(jnp.int32, (n2, n2), 1)).astype(F32)

    def head_sum(x):
        wide = ones_ref.shape[0]
        xs = jnp.concatenate([x[:, j:j + wide] for j in range(0, d, wide)], axis=0)
        ss = _dot(xs.astype(BF16), ones_ref[...])
        return jnp.concatenate([ss[j * c:(j + 1) * c, :] for j in range(d // wide)], axis=1)

    def prep(q, rows, p):
        r = rkv_ref[q, rows, 0:d]
        k = rkv_ref[q, rows, d:2 * d]
        v = rkv_ref[q, rows, 2 * d:3 * d]
        lo_wa = lo_ref[q, rows, 0:V7X_LANES]
        w_pre = w0 + _dot(jnp.tanh(lo_wa).astype(BF16), wd_ref[...])
        a_sig = jax.nn.sigmoid(a0 + _dot(lo_wa.astype(BF16), wa_ref[...]))
        p["gate"] = _dot(jax.nn.sigmoid(lo_ref[q, rows, V7X_LANES:2 * V7X_LANES]).astype(BF16), wg_ref[...])
        if has_vres:
            mix = jax.nn.sigmoid(v0 + _dot(lo_ref[q, rows, 2 * V7X_LANES:3 * V7X_LANES].astype(BF16), wv_ref[...]))
            v = v + (vfirst_ref[q, rows, :] - v) * mix
        yield
        z = -w_pre
        softplus = jnp.maximum(z, 0.0) + jnp.log1p(jnp.exp(-jnp.abs(z)))
        logw = -jnp.exp(-softplus - 0.5)
        kk = k * k_k
        kk = kk * jnp.minimum(lax.rsqrt(head_sum(kk * kk)), 1e12)
        k = k * (1.0 + (a_sig - 1.0) * k_a)
        b_vec = kk * a_sig
        p["r"], p["k"], p["v"] = r, k, v
        yield
        tri = tri_ref[...]
        lh, lm, ll = _split3(logw)
        cum = _dot(tri, lh) + _dot(tri, lm) + _dot(tri, ll)
        tot = cum[c - 1:c, :]
        g_tot = jnp.exp(tot)
        p["g_tot"] = g_tot
        yield
        e_inv = jnp.exp(-cum)
        stack = lambda x: [_stack_heads(x.astype(BF16)[:, sl], lane_is_head0) for sl in lanes]
        p["a_s"] = stack(-kk * jnp.exp(cum - logw))
        yield
        p["r_s"] = stack(r * jnp.exp(cum))
        yield
        b_t = b_vec * e_inv
        p["b_s"] = stack(b_t)
        yield
        k_t = k * e_inv
        p["k_s"] = stack(k_t)
        yield
        p["v_s"] = stack(v)
        yield
        p["bh_s"] = stack(b_t * g_tot)
        yield
        p["kh_s"] = stack(k_t * g_tot)

    def mix(q, rows, p):
        a_s, r_s, b_s, k_s, v_s = p["a_s"], p["r_s"], p["b_s"], p["k_s"], p["v_s"]
        zz = [_dot_nt(jnp.concatenate([a_s[s], r_s[s]], axis=0), jnp.concatenate([b_s[s], k_s[s]], axis=0))
              for s in slabs]
        yield
        a_ab = [jnp.where(strict, z[0:n2, 0:n2], 0.0) for z in zz]
        a_ak = [jnp.where(strict, z[0:n2, n2:2 * n2], 0.0).astype(BF16) for z in zz]
        m_rb = [jnp.where(incl, z[n2:2 * n2, 0:n2], 0.0).astype(BF16) for z in zz]
        m_rk = [jnp.where(incl, z[n2:2 * n2, n2:2 * n2], 0.0).astype(BF16) for z in zz]
        pb = [a.astype(BF16) for a in a_ab]
        t_inv = [eye + a for a in a_ab]
        pw = [_dot(m, m) for m in pb]
        yield
        akv = [_dot(a_ak[s], v_s[s]).astype(BF16) for s in slabs]
        sb = [state_ref[q, s].astype(BF16) for s in slabs]
        y_rs = [_dot_nt(r_s[s], sb[s]) + _dot(m_rk[s], v_s[s]) for s in slabs]
        yield
        steps = CHUNK.bit_length() - 2
        for it in range(steps):
            pb = [m.astype(BF16) for m in pw]
            if it + 1 < steps:
                both = [_dot(pb[s], jnp.concatenate([t_inv[s].astype(BF16), pb[s]], axis=1)) for s in slabs]
                t_inv = [t_inv[s] + both[s][:, 0:n2] for s in slabs]
                pw = [b2[:, n2:2 * n2] for b2 in both]
            else:
                t_inv = [t_inv[s] + _dot(pb[s], t_inv[s].astype(BF16)) for s in slabs]
            yield
        wu = [_dot(t_inv[s].astype(BF16), jnp.concatenate([a_s[s], akv[s]], axis=1)) for s in slabs]
        yield
        ub = [(_dot_nt(wu[s][:, 0:V7X_LANES].astype(BF16), sb[s]) + wu[s][:, V7X_LANES:2 * V7X_LANES]).astype(BF16)
              for s in slabs]
        yield
        for s in slabs:
            y_s = y_rs[s] + _dot(m_rb[s], ub[s])
            y_ref[q, rows, lanes[s]] = y_s[0:c, :] + y_s[c:2 * c, :]
        yield
        for s in slabs:
            state_ref[q, s] = state_ref[q, s] * p["g_tot"][:, lanes[s]] + _dot_tn(
                jnp.concatenate([ub[s], v_s[s]], axis=0), jnp.concatenate([p["bh_s"][s], p["kh_s"][s]], axis=0))

    def post(q, rows, p):
        y = y_ref[q, rows, :]
        inv_n = 1.0 / HEAD_SIZE
        mu = head_sum(y) * inv_n
        yield
        yc = y - mu
        var = head_sum(yc * yc) * inv_n
        yield
        yn = yc * lax.rsqrt(var + GN_EPS) * gn_gain + gn_bias
        bonus = head_sum(p["r"] * p["k"] * r_k) * p["v"]
        out_ref[q, rows, :] = (yn + bonus) * p["gate"]

    def lockstep(gens):
        live = list(gens)
        while live:
            live = [g for g in live if next(g, StopIteration) is not StopIteration]
            if live:
                yield

    def drive(main, fillers):
        for _ in main:
            for f in fillers:
                next(f, None)
        for f in fillers:
            for _ in f:
                pass

    streams = range(SEQS_PER_STEP)
    row_slices = [slice(i * c, (i + 1) * c) for i in range(CHUNKS_PER_STEP)]
    params = [[dict() for _ in row_slices] for _ in streams]
    drive(lockstep([prep(q, row_slices[0], params[q][0]) for q in streams]), [])
    for i, rows in enumerate(row_slices):
        fillers = []
        if i + 1 < len(row_slices):
            fillers += [prep(q, row_slices[i + 1], params[q][i + 1]) for q in streams]
        if i > 0:
            fillers += [post(q, row_slices[i - 1], params[q][i - 1]) for q in streams]
        drive(lockstep([mix(q, rows, params[q][i]) for q in streams]), fillers)
    drive(lockstep([post(q, row_slices[-1], params[q][-1]) for q in streams]), [])


def _rwkv_time_mix(rkv, lo, v_first_src, pvec, wd, wa, wg, wv, tri, ones_bd, *, bsz, seq, d):
    has_vres = v_first_src is not None
    step_rows = CHUNK * CHUNKS_PER_STEP
    steps = seq // step_rows
    n_slab = d // V7X_LANES
    nq = SEQS_PER_STEP
    blk = lambda width, col: pl.BlockSpec((nq, step_rows, width), lambda b, t: (b, t, col))
    in_specs = [blk(3 * d, 0), blk(LO_WIDTH, 0)]
    args = [rkv.reshape(bsz, seq, 3 * d), lo.reshape(bsz, seq, LO_WIDTH)]
    if has_vres:
        in_specs.append(blk(d, 2))
        args.append(v_first_src.reshape(bsz, seq, 3 * d))
    consts = [pvec, wd, wa, wg] + ([wv] if has_vres else []) + [tri, ones_bd]
    in_specs += [_const_spec(a.shape) for a in consts]
    args += consts
    out = pl.pallas_call(
        functools.partial(_rwkv_kernel, d=d, has_vres=has_vres),
        grid=(bsz // nq, steps),
        in_specs=in_specs,
        out_specs=blk(d, 0),
        out_shape=jax.ShapeDtypeStruct((bsz, seq, d), F32),
        scratch_shapes=[pltpu.VMEM((nq, n_slab, V7X_LANES, V7X_LANES), F32), pltpu.VMEM((nq, step_rows, d), F32)],
        compiler_params=pltpu.CompilerParams(dimension_semantics=("arbitrary", "arbitrary"),
                                             vmem_limit_bytes=VMEM_LIMIT),
        name="rwkv7_time_mix",
    )(*args)
    return out.reshape(bsz * seq, d)


def _mix_mlp_kernel(ya_ref, z_ref, gate_ref, x_ref, mod_ref, cvec_ref, wout_ref, w1_ref, w2_ref, out_ref,
                    *, d, final_norm):
    tm = x_ref.shape[0]
    ln_gain, ln_bias = cvec_ref[0:1, :], cvec_ref[1:2, :]
    ffn_gain, final_gain = cvec_ref[2:3, :], cvec_ref[3:4, :]
    mod = mod_ref[0]
    gt_m = mod[:, 2 * d:3 * d]
    sh_f, sc_f, gt_f = mod[:, 3 * d:4 * d], mod[:, 4 * d:5 * d], mod[:, 5 * d:6 * d]

    m_a = (ya_ref[...] * gate_ref[:, 0:d]).astype(BF16)
    o_a = _dot(m_a, wout_ref[0:d, :])
    z = z_ref[...]
    mu = jnp.mean(z, axis=-1, keepdims=True)
    zc = z - mu
    var = jnp.mean(zc * zc, axis=-1, keepdims=True)
    zn = zc * lax.rsqrt(var + LN_EPS) * ln_gain + ln_bias
    y_b = zn * jax.nn.sigmoid(zn)
    m_b = (y_b * gate_ref[:, d:2 * d]).astype(BF16)
    x1 = x_ref[...] + gt_m * (o_a + _dot(m_b, wout_ref[d:2 * d, :]))

    ms = jnp.mean(x1 * x1, axis=-1, keepdims=True)
    h = ((x1 * lax.rsqrt(ms + RMS_EPS)) * ffn_gain * (1.0 + sc_f) + sh_f).astype(BF16)
    d_ff = w1_ref.shape[1]
    acc = jnp.zeros((tm, d), F32)
    for j in range(d_ff // d):
        hid = jnp.maximum(_dot(h, w1_ref[:, j * d:(j + 1) * d]), 0.0)
        acc = acc + _dot((hid * hid).astype(BF16), w2_ref[j * d:(j + 1) * d, :])
    x2 = x1 + gt_f * acc
    if final_norm:
        ms2 = jnp.mean(x2 * x2, axis=-1, keepdims=True)
        x2 = (x2 * lax.rsqrt(ms2 + RMS_EPS)) * final_gain
    out_ref[...] = x2


def _mix_mlp(ya, z, gates, x2, mod_l, cvec, wout, w1, w2, layer, *, seq, d, final_norm):
    rows = x2.shape[0]
    tm = min(MLP_ROW_TILE, seq)
    tiles_per_seq = seq // tm
    row_spec = lambda width: pl.BlockSpec((tm, width), lambda i: (i, 0))
    return pl.pallas_call(
        functools.partial(_mix_mlp_kernel, d=d, final_norm=final_norm),
        grid=(rows // tm,),
        in_specs=[
            row_spec(d), row_spec(d), row_spec(2 * d), row_spec(d),
            pl.BlockSpec((1, 1, mod_l.shape[-1]), lambda i: (i // tiles_per_seq, 0, 0)),
            _const_spec(cvec.shape), _layer_spec(wout.shape, layer), _layer_spec(w1.shape, layer),
            _layer_spec(w2.shape, layer),
        ],
        out_specs=row_spec(d),
        out_shape=jax.ShapeDtypeStruct((rows, d), F32),
        compiler_params=pltpu.CompilerParams(dimension_semantics=("arbitrary",), vmem_limit_bytes=VMEM_LIMIT),
        name="merge_mlp",
    )(ya, z, gates, x2, mod_l, cvec, wout, w1, w2)


def _pad_rows(w, rows, row0=0):
    out = jnp.zeros((rows, w.shape[1]), w.dtype)
    return lax.dynamic_update_slice(out, w, (row0, 0))


def kernel(x, c, norm_mix_gain, norm_ffn_gain, ada_w, ada_b, w_in, w_in_vres, mu_shift, mu_vres, w0, w_decay_up, a0, w_aaa_up, w_gate_up, k_k, k_a, r_k, gn_gain, gn_bias, v0, w_vres_up, conv_w, conv_b, conv_ln_gain, conv_ln_bias, w_out, w_ff_in, w_ff_out, final_gain):
    bsz, seq, d = x.shape
    depth = w_in.shape[0]
    n_rkv = 3 * d
    n_lo = DECAY_RANK + AAA_RANK + GATE_RANK
    n_shift = n_rkv + n_lo
    n_gate = 2 * d
    assert bsz % SEQS_PER_STEP == 0 and d % V7X_MXU_WIDTH == 0 and seq % (CHUNK * CHUNKS_PER_STEP) == 0 and seq % min(ROW_TILE, seq) == 0 and seq % min(MLP_ROW_TILE, seq) == 0
    assert w_in.shape[2] == n_shift + 2 * d + n_gate and ada_w.shape[2] % MOD_COL_TILE == 0

    mod = _modulation(c, ada_w, ada_b)
    tri = jnp.tril(jnp.ones((CHUNK, CHUNK), BF16))
    lane = jnp.arange(V7X_MXU_WIDTH) // HEAD_SIZE
    ones_bd = (lane[:, None] == lane[None, :]).astype(BF16)

    lo_pad = LO_WIDTH - n_lo
    w_tail_b = jnp.concatenate([jnp.zeros((1, d, lo_pad), F32),
                                jnp.pad(w_in_vres, ((0, 0), (0, 0), (0, lo_pad - VRES_RANK)))], axis=0).astype(BF16)
    mu_tail = jnp.concatenate([jnp.zeros((1, lo_pad), F32), jnp.pad(mu_vres, ((0, 0), (0, lo_pad - VRES_RANK)))], axis=0)
    w_in_b = w_in.astype(BF16)
    mu_all = jnp.concatenate([mu_shift, mu_tail], axis=1)
    w_out_b, w_ff_in_b, w_ff_out_b = w_out.astype(BF16), w_ff_in.astype(BF16), w_ff_out.astype(BF16)

    x2 = x.reshape(bsz * seq, d)
    v_first_src = None
    for l in range(depth):
        has_vres = l > 0
        mod_l = mod[l].reshape(bsz, 1, 6 * d)
        rkv, lo, z, gates = _in_projection(x2, mod_l, norm_mix_gain[l].reshape(1, d), mu_all[l].reshape(1, -1),
                                           _pad_rows(conv_w[l], CONV_HALO), conv_b[l].reshape(1, d), w_in_b, w_tail_b, l,
                                           seq=seq, d=d, n_rkv=n_rkv, n_gate=n_gate)

        pvec = jnp.stack([w0[l], a0[l], k_k[l], k_a[l], r_k[l].reshape(d), gn_gain[l], gn_bias[l],
                          v0[l - 1] if has_vres else jnp.zeros((d,), F32)])
        wd = _pad_rows(w_decay_up[l], V7X_LANES, 0).astype(BF16)
        wa = _pad_rows(w_aaa_up[l], V7X_LANES, DECAY_RANK).astype(BF16)
        wg = w_gate_up[l].astype(BF16)
        wv = _pad_rows(w_vres_up[l - 1], V7X_LANES, 0).astype(BF16) if has_vres else None
        ya = _rwkv_time_mix(rkv, lo, v_first_src, pvec, wd, wa, wg, wv, tri, ones_bd, bsz=bsz, seq=seq, d=d)
        if l == 0:
            v_first_src = rkv

        cvec = jnp.stack([conv_ln_gain[l], conv_ln_bias[l], norm_ffn_gain[l], final_gain] + [jnp.zeros((d,), F32)] * 4)
        x2 = _mix_mlp(ya, z, gates, x2, mod_l, cvec, w_out_b, w_ff_in_b, w_ff_out_b, l,
                      seq=seq, d=d, final_norm=(l == depth - 1))
    return x2.reshape(bsz, seq, d)
```
